```python
import jax, jax.numpy as jnp
from jax import lax
import numpy as np

D_MODEL = 2048
BATCH = 4
SEQ = 4096
DEPTH = 1

CHUNK = 64
LEFT_CHUNKS = 8
BAND = (LEFT_CHUNKS + 1) * CHUNK
LEFT = LEFT_CHUNKS * CHUNK
HEAD_DIM = 64
ATTN_HEADS = D_MODEL // 128
ATTN_WIDTH = ATTN_HEADS * HEAD_DIM
MAX_REL = 256
CONV_WIDTH = D_MODEL // 2
CONV_KERNEL = 31
IN_WIDTH = 3 * ATTN_WIDTH + 2 * CONV_WIDTH + 2 * D_MODEL
IN_SPLITS = (ATTN_WIDTH, 2 * ATTN_WIDTH, 3 * ATTN_WIDTH,
             3 * ATTN_WIDTH + CONV_WIDTH, 3 * ATTN_WIDTH + 2 * CONV_WIDTH,
             3 * ATTN_WIDTH + 2 * CONV_WIDTH + D_MODEL)
N_EXPERTS = 32
TOP_K = 4
D_EXPERT = D_MODEL
SWIGLU_LIMIT = 7.0
SWIGLU_ALPHA = 1.702
MOE_BLOCK = 128
RMS_EPS = 1e-6
LN_EPS = 1e-5

kernel_name = "hybrid_chunk_attn_conformer_conv_moe_block"


def rms_norm(x, g):
    xf = x.astype(jnp.float32)
    y = xf * lax.rsqrt(jnp.mean(xf * xf, axis=-1, keepdims=True) + RMS_EPS)
    return (y * g.astype(jnp.float32)).astype(x.dtype)


def layer_norm(x, g, b):
    xf = x.astype(jnp.float32)
    mu = jnp.mean(xf, axis=-1, keepdims=True)
    var = jnp.mean(jnp.square(xf - mu), axis=-1, keepdims=True)
    y = (xf - mu) * lax.rsqrt(var + LN_EPS)
    return (y * g.astype(jnp.float32) + b.astype(jnp.float32)).astype(x.dtype)


def chunk_band_attention(q, k, v, rel_bias):
    B, S, H, Dh = q.shape
    nc = S // CHUNK
    kp = jnp.pad(k, ((0, 0), (LEFT, 0), (0, 0), (0, 0)))
    vp = jnp.pad(v, ((0, 0), (LEFT, 0), (0, 0), (0, 0)))
    qc = q.reshape(B, nc, CHUNK, H, Dh).transpose(1, 0, 2, 3, 4)
    rel = LEFT + jnp.arange(CHUNK)[:, None] - jnp.arange(BAND)[None, :]
    rel_idx = jnp.clip(rel, -MAX_REL, MAX_REL) + MAX_REL
    bias = rel_bias[:, rel_idx].astype(jnp.float32)
    scale = HEAD_DIM ** -0.5
    neg = jnp.finfo(jnp.float32).min

    def one_chunk(args):
        qb, i = args
        start = i * CHUNK
        kb = lax.dynamic_slice_in_dim(kp, start, BAND, axis=1)
        vb = lax.dynamic_slice_in_dim(vp, start, BAND, axis=1)
        s = jnp.einsum('bqhd,bkhd->bhqk', qb, kb,
                       preferred_element_type=jnp.float32) * scale + bias
        valid = (start - LEFT + jnp.arange(BAND)) >= 0
        s = jnp.where(valid, s, neg)
        p = jax.nn.softmax(s, axis=-1)
        return jnp.einsum('bhqk,bkhd->bqhd', p.astype(vb.dtype), vb)

    out = lax.map(one_chunk, (qc, jnp.arange(nc)))
    return out.transpose(1, 0, 2, 3, 4).reshape(B, S, H, Dh)


def conformer_conv(a, b, w_dw, b_dw, ln_g, ln_b, w_conv_out):
    u = a * jax.nn.sigmoid(b)
    u = lax.conv_general_dilated(
        u, w_dw[:, None, :], window_strides=(1,),
        padding=[(CONV_KERNEL - 1, 0)],
        dimension_numbers=('NWC', 'WIO', 'NWC'),
        feature_group_count=CONV_WIDTH) + b_dw
    u = jax.nn.silu(layer_norm(u, ln_g, ln_b))
    return u @ w_conv_out


def hybrid_mixer(h, w_in, rel_bias, w_attn_out, w_dw, b_dw, ln_g, ln_b, w_conv_out, w_out):
    B, S, _ = h.shape
    proj = h @ w_in
    q, k, v, ca, cb, ga, gc = jnp.split(proj, IN_SPLITS, axis=-1)
    hs = (B, S, ATTN_HEADS, HEAD_DIM)
    attn = chunk_band_attention(q.reshape(hs), k.reshape(hs), v.reshape(hs), rel_bias)
    y_attn = attn.reshape(B, S, ATTN_WIDTH) @ w_attn_out
    y_conv = conformer_conv(ca, cb, w_dw, b_dw, ln_g, ln_b, w_conv_out)
    merged = jax.nn.sigmoid(ga) * y_attn + jax.nn.sigmoid(gc) * y_conv
    return merged @ w_out


def moe_ffn(h, w_router, b_router, w_gate_up, b_gate_up, w_down, b_down):
    N, D = h.shape
    logits = (h @ w_router + b_router).astype(jnp.float32)
    top_val, top_idx = lax.top_k(logits, TOP_K)
    probs = jax.nn.softmax(top_val, axis=-1)
    A = N * TOP_K
    flat_e = top_idx.reshape(A)
    flat_tok = jnp.repeat(jnp.arange(N, dtype=jnp.int32), TOP_K)
    flat_p = probs.reshape(A)
    order = jnp.argsort(flat_e)
    e_sorted = flat_e[order]
    counts = jnp.bincount(flat_e, length=N_EXPERTS)
    padded = ((counts + MOE_BLOCK - 1) // MOE_BLOCK) * MOE_BLOCK
    padded_end = jnp.cumsum(padded)
    padded_start = padded_end - padded
    start = jnp.cumsum(counts) - counts
    dest = padded_start[e_sorted] + (jnp.arange(A) - start[e_sorted])
    P = A + N_EXPERTS * MOE_BLOCK
    nb = P // MOE_BLOCK
    row_tok = jnp.zeros((P,), jnp.int32).at[dest].set(flat_tok[order])
    row_w = jnp.zeros((P,), jnp.float32).at[dest].set(flat_p[order])
    block_e = jnp.minimum(
        jnp.searchsorted(padded_end, jnp.arange(nb) * MOE_BLOCK, side='right'),
        N_EXPERTS - 1)

    def expert_block(args):
        tok, e = args
        xb = h[tok]
        gu = xb @ w_gate_up[e] + b_gate_up[e]
        gate, up = jnp.split(gu, 2, axis=-1)
        gate = jnp.minimum(gate, SWIGLU_LIMIT)
        up = jnp.clip(up, -SWIGLU_LIMIT, SWIGLU_LIMIT)
        glu = gate * jax.nn.sigmoid(SWIGLU_ALPHA * gate)
        return ((up + 1) * glu) @ w_down[e] + b_down[e]

    y_rows = lax.map(expert_block, (row_tok.reshape(nb, MOE_BLOCK), block_e)).reshape(P, D)
    out = jax.ops.segment_sum(y_rows * row_w[:, None].astype(y_rows.dtype), row_tok,
                              num_segments=N)
    return out.astype(h.dtype)


def setup_inputs(seed: int = 0) -> dict:
    key = jax.random.key(seed)
    ks = jax.random.split(key, 24)
    f32 = jnp.float32
    L = DEPTH

    def nrm(k, shape, scale):
        return jax.random.normal(k, shape, f32) * scale

    return {
        "x": nrm(ks[0], (BATCH, SEQ, D_MODEL), 1.0),
        "c": nrm(ks[1], (BATCH, D_MODEL), 1.0),
        "norm_mix_pre": 1.0 + nrm(ks[2], (L, D_MODEL), 0.05),
        "norm_mix_post": 1.0 + nrm(ks[3], (L, D_MODEL), 0.05),
        "norm_ffn_pre": 1.0 + nrm(ks[4], (L, D_MODEL), 0.05),
        "norm_ffn_post": 1.0 + nrm(ks[5], (L, D_MODEL), 0.05),
        "w_ada": nrm(ks[6], (L, D_MODEL, 6 * D_MODEL), D_MODEL ** -0.5),
        "b_ada": nrm(ks[7], (L, 6 * D_MODEL), 0.02),
        "w_in": nrm(ks[8], (L, D_MODEL, IN_WIDTH), D_MODEL ** -0.5),
        "rel_bias": nrm(ks[9], (L, ATTN_HEADS, 2 * MAX_REL + 1), 0.5),
        "w_attn_out": nrm(ks[10], (L, ATTN_WIDTH, D_MODEL), ATTN_WIDTH ** -0.5),
        "w_dw": nrm(ks[11], (L, CONV_KERNEL, CONV_WIDTH), CONV_KERNEL ** -0.5),
        "b_dw": nrm(ks[12], (L, CONV_WIDTH), 0.02),
        "conv_ln_g": 1.0 + nrm(ks[13], (L, CONV_WIDTH), 0.05),
        "conv_ln_b": nrm(ks[14], (L, CONV_WIDTH), 0.02),
        "w_conv_out": nrm(ks[15], (L, CONV_WIDTH, D_MODEL), CONV_WIDTH ** -0.5),
        "w_out": nrm(ks[16], (L, D_MODEL, D_MODEL), D_MODEL ** -0.5),
        "w_router": nrm(ks[17], (L, D_MODEL, N_EXPERTS), D_MODEL ** -0.5),
        "b_router": nrm(ks[18], (L, N_EXPERTS), 0.01),
        "w_gate_up": nrm(ks[19], (L, N_EXPERTS, D_MODEL, 2 * D_EXPERT), D_MODEL ** -0.5),
        "b_gate_up": nrm(ks[20], (L, N_EXPERTS, 2 * D_EXPERT), 0.02),
        "w_down": nrm(ks[21], (L, N_EXPERTS, D_EXPERT, D_MODEL), D_EXPERT ** -0.5),
        "b_down": nrm(ks[22], (L, N_EXPERTS, D_MODEL), 0.02),
    }


def reference(x, c, norm_mix_pre, norm_mix_post, norm_ffn_pre, norm_ffn_post,
              w_ada, b_ada, w_in, rel_bias, w_attn_out, w_dw, b_dw, conv_ln_g, conv_ln_b,
              w_conv_out, w_out, w_router, b_router, w_gate_up, b_gate_up, w_down, b_down):
    B, S, D = x.shape
    for l in range(DEPTH):
        mod = jax.nn.silu(c) @ w_ada[l] + b_ada[l]
        sh1, sc1, g1, sh2, sc2, g2 = jnp.split(mod[:, None, :], 6, axis=-1)
        h = rms_norm(x, norm_mix_pre[l]) * (1 + sc1) + sh1
        y = hybrid_mixer(h, w_in[l], rel_bias[l], w_attn_out[l], w_dw[l], b_dw[l],
                         conv_ln_g[l], conv_ln_b[l], w_conv_out[l], w_out[l])
        x = x + g1 * rms_norm(y, norm_mix_post[l])
        h = rms_norm(x, norm_ffn_pre[l]) * (1 + sc2) + sh2
        y = moe_ffn(h.reshape(B * S, D), w_router[l], b_router[l], w_gate_up[l],
                    b_gate_up[l], w_down[l], b_down[l]).reshape(B, S, D)
        x = x + g2 * rms_norm(y, norm_ffn_post[l])
    return x
```

```python
import functools

import jax
import jax.numpy as jnp
from jax import lax
from jax.experimental import pallas as pl
from jax.experimental.pallas import tpu as pltpu

F32 = jnp.float32
BF16 = jnp.bfloat16
I32 = jnp.int32
U32 = jnp.uint32

CHUNK = 64
LEFT_CHUNKS = 8
HEAD_DIM = 64
TOP_K = 4
SWIGLU_LIMIT = 7.0
SWIGLU_ALPHA = 1.702
RMS_EPS = 1e-6
LN_EPS = 1e-5
MASK_VALUE = -1e30

LANES = 128
SUBLANES = 8
VMEM_LIMIT_BYTES = 56 * 1024 * 1024

ADA_TN = 1024
INPROJ_TM = 1024
INPROJ_TN = 1024
ATTN_TQ = 256
CONV_TS = 128
CONV_CG = 256
CONV_HALO = 32
MIX_TM = 256
TOPK_TM = 1024
SLOT_TM = 512
MOE_TM = 256
MOE_TN = 512
ROW_TT = 256


def _params(sem, vmem=VMEM_LIMIT_BYTES):
    return pltpu.CompilerParams(dimension_semantics=sem, vmem_limit_bytes=vmem)


def _rms(x, eps):
    return x * lax.rsqrt(jnp.mean(x * x, axis=-1, keepdims=True) + eps)


def _ada_kernel(c_ref, w_ref, b_ref, o_ref):
    c = c_ref[...]
    s = c * jax.nn.sigmoid(c)
    o_ref[...] = jnp.dot(s.astype(BF16), w_ref[...].astype(BF16),
                         preferred_element_type=F32) + b_ref[...]


def _ada(c_pad, w_ada, b_ada):
    rows, d = c_pad.shape
    n_out = w_ada.shape[1]
    return pl.pallas_call(
        _ada_kernel,
        grid=(n_out // ADA_TN,),
        in_specs=[pl.BlockSpec((rows, d), lambda j: (0, 0)),
                  pl.BlockSpec((d, ADA_TN), lambda j: (0, j)),
                  pl.BlockSpec((1, ADA_TN), lambda j: (0, j))],
        out_specs=pl.BlockSpec((rows, ADA_TN), lambda j: (0, j)),
        out_shape=jax.ShapeDtypeStruct((rows, n_out), F32),
        compiler_params=_params(("arbitrary",)),
        name="ada",
    )(c_pad, w_ada, b_ada)


def _inproj_kernel(x_ref, g_ref, sc_ref, sh_ref, w_ref, o_ref, h_ref):
    @pl.when(pl.program_id(1) == 0)
    def _():
        y = _rms(x_ref[...], RMS_EPS) * g_ref[...]
        h_ref[...] = (y * (1.0 + sc_ref[...]) + sh_ref[...]).astype(BF16)

    o_ref[...] = jnp.dot(h_ref[...], w_ref[...], preferred_element_type=F32).astype(BF16)


def _inproj(x2, g, sc, sh, w_bf16, seq):
    n, d = x2.shape
    width = w_bf16.shape[1]
    tm, tn = INPROJ_TM, INPROJ_TN
    mod_spec = pl.BlockSpec((None, 1, d), lambda i, j: ((i * tm) // seq, 0, 0))
    return pl.pallas_call(
        _inproj_kernel,
        grid=(n // tm, width // tn),
        in_specs=[pl.BlockSpec((tm, d), lambda i, j: (i, 0)),
                  pl.BlockSpec((1, d), lambda i, j: (0, 0)),
                  mod_spec, mod_spec,
                  pl.BlockSpec((d, tn), lambda i, j: (0, j))],
        out_specs=pl.BlockSpec((tm, tn), lambda i, j: (i, j)),
        out_shape=jax.ShapeDtypeStruct((n, width), BF16),
        scratch_shapes=[pltpu.VMEM((tm, d), BF16)],
        compiler_params=_params(("arbitrary", "arbitrary")),
        name="inproj",
    )(x2, g, sc, sh, w_bf16)


def _attn_kernel(q_ref, k0_ref, k1_ref, k2_ref, v0_ref, v1_ref, v2_ref, bias_ref, o_ref):
    q = q_ref[...]
    k = jnp.concatenate([k0_ref[...], k1_ref[...], k2_ref[...]], axis=0)
    v = jnp.concatenate([v0_ref[...], v1_ref[...], v2_ref[...]], axis=0)
    scale = HEAD_DIM ** -0.5
    outs = []
    for hh in range(LANES // HEAD_DIM):
        sl = slice(hh * HEAD_DIM, (hh + 1) * HEAD_DIM)
        s = lax.dot_general(q[:, sl], k[:, sl], (((1,), (1,)), ((), ())),
                            preferred_element_type=F32)
        s = s * scale + bias_ref[hh]
        m = jnp.max(s, axis=-1, keepdims=True)
        p = jnp.exp(s - m)
        l = jnp.sum(p, axis=-1, keepdims=True)
        o = jnp.dot(p.astype(BF16), v[:, sl], preferred_element_type=F32)
        outs.append(o / l)
    o_ref[...] = jnp.concatenate(outs, axis=1).astype(BF16)


def _attn_bias(rel_bias, tq, left, max_rel):
    win = left + tq
    r = jnp.arange(tq)[:, None]
    c = jnp.arange(win)[None, :]
    dist = (left + r) - c
    table = rel_bias[:, jnp.clip(dist, -max_rel, max_rel) + max_rel].astype(F32)
    qc = r // CHUNK
    kc = c // CHUNK
    band = (kc >= qc) & (kc <= qc + LEFT_CHUNKS)
    variants = []
    for var in range(left // tq + 1):
        valid = band & (c >= left - var * tq)
        variants.append(jnp.where(valid[None], table, MASK_VALUE))
    return jnp.stack(variants)


def _attention(proj, bias, batch, seq, attn_width, q_off):
    n = proj.shape[0]
    tq = ATTN_TQ
    nq = seq // tq
    pairs = attn_width // LANES
    qcol = q_off // LANES
    kcol = (q_off + attn_width) // LANES
    vcol = (q_off + 2 * attn_width) // LANES
    nback = LEFT_CHUNKS * CHUNK // tq
    nvar = bias.shape[0]
    heads_per = LANES // HEAD_DIM

    def kv_spec(col0, back):
        return pl.BlockSpec(
            (tq, LANES),
            lambda hp, b, qi: (b * nq + jnp.maximum(qi - back, 0), col0 + hp))

    in_specs = [pl.BlockSpec((tq, LANES), lambda hp, b, qi: (b * nq + qi, qcol + hp))]
    in_specs += [kv_spec(kcol, back) for back in range(nback, -1, -1)]
    in_specs += [kv_spec(vcol, back) for back in range(nback, -1, -1)]
    in_specs += [pl.BlockSpec((None, heads_per, tq, bias.shape[-1]),
                              lambda hp, b, qi: (jnp.minimum(qi, nvar - 1), hp, 0, 0))]
    return pl.pallas_call(
        _attn_kernel,
        grid=(pairs, batch, nq),
        in_specs=in_specs,
        out_specs=pl.BlockSpec((tq, LANES), lambda hp, b, qi: (b * nq + qi, hp)),
        out_shape=jax.ShapeDtypeStruct((n, attn_width), BF16),
        compiler_params=_params(("arbitrary", "arbitrary", "arbitrary")),
        name="attn",
    )(proj, *([proj] * (2 * (nback + 1))), bias)


def _conv_kernel(ca_ref, cb_ref, w_ref, b_ref, g_ref, beta_ref, o_ref, ubuf, cbuf, *, taps):
    ts = ca_ref.shape[0]
    width = ca_ref.shape[1]
    halo = CONV_HALO
    first = pl.program_id(1) == 0

    @pl.when(first)
    def _():
        ubuf[0:halo, :] = jnp.zeros((halo, width), F32)

    @pl.when(jnp.logical_not(first))
    def _():
        ubuf[0:halo, :] = ubuf[ts:ts + halo, :]

    ubuf[halo:halo + ts, :] = (ca_ref[...].astype(F32)
                               * jax.nn.sigmoid(cb_ref[...].astype(F32)))
    off = halo - (taps - 1)
    for cg in range(width // CONV_CG):
        cs = slice(cg * CONV_CG, (cg + 1) * CONV_CG)
        acc = jnp.zeros((ts, CONV_CG), F32)
        for j in range(taps):
            acc = acc + w_ref[j:j + 1, cs] * ubuf[off + j:off + j + ts, cs]
        cbuf[:, cs] = acc + b_ref[:, cs]
    y = cbuf[...]
    mu = jnp.mean(y, axis=-1, keepdims=True)
    yc = y - mu
    var = jnp.mean(yc * yc, axis=-1, keepdims=True)
    yn = yc * lax.rsqrt(var + LN_EPS) * g_ref[...] + beta_ref[...]
    o_ref[...] = (yn * jax.nn.sigmoid(yn)).astype(BF16)


def _conv(proj, w_dw_pad, b_dw, ln_g, ln_b, batch, seq, col_a, col_b, width, taps):
    n = proj.shape[0]
    ts = CONV_TS
    ns = seq // ts
    vec = pl.BlockSpec((1, width), lambda b, i: (0, 0))
    return pl.pallas_call(
        functools.partial(_conv_kernel, taps=taps),
        grid=(batch, ns),
        in_specs=[pl.BlockSpec((ts, width), lambda b, i: (b * ns + i, col_a)),
                  pl.BlockSpec((ts, width), lambda b, i: (b * ns + i, col_b)),
                  pl.BlockSpec(w_dw_pad.shape, lambda b, i: (0, 0)),
                  vec, vec, vec],
        out_specs=pl.BlockSpec((ts, width), lambda b, i: (b * ns + i, 0)),
        out_shape=jax.ShapeDtypeStruct((n, width), BF16),
        scratch_shapes=[pltpu.VMEM((CONV_HALO + ts, width), F32),
                        pltpu.VMEM((ts, width), F32)],
        compiler_params=_params(("arbitrary", "arbitrary")),
        name="conv",
    )(proj, proj, w_dw_pad, b_dw, ln_g, ln_b)


def _pack_bf16_pair(lo, hi):
    lo_bits = lax.bitcast_convert_type(lo.astype(BF16).astype(F32), U32)
    hi_bits = lax.bitcast_convert_type(hi.astype(BF16).astype(F32), U32)
    return (lo_bits >> 16) | (hi_bits & jnp.uint32(0xFFFF0000))


def _unpack_bf16_pair(packed):
    lo = lax.bitcast_convert_type(packed << 16, F32).astype(BF16)
    hi = lax.bitcast_convert_type(packed & jnp.uint32(0xFFFF0000), F32).astype(BF16)
    return lo, hi


def _mix_kernel(attn_ref, cv_ref, ga_ref, gc_ref, x_ref, wa_ref, wc_ref, wo_ref,
                gpost_ref, g1_ref, gpre_ref, sc_ref, sh_ref, wr_ref, br_ref,
                x1_ref, hp_ref, lg_ref):
    ya = jnp.dot(attn_ref[...], wa_ref[...], preferred_element_type=F32)
    yc = jnp.dot(cv_ref[...], wc_ref[...], preferred_element_type=F32)
    merged = (jax.nn.sigmoid(ga_ref[...].astype(F32)) * ya
              + jax.nn.sigmoid(gc_ref[...].astype(F32)) * yc)
    y = jnp.dot(merged.astype(BF16), wo_ref[...], preferred_element_type=F32)
    x1 = x_ref[...] + g1_ref[...] * (_rms(y, RMS_EPS) * gpost_ref[...])
    x1_ref[...] = x1
    h = _rms(x1, RMS_EPS) * gpre_ref[...] * (1.0 + sc_ref[...]) + sh_ref[...]
    half = h.shape[1] // 2
    hp_ref[...] = _pack_bf16_pair(h[:, :half], h[:, half:])
    lg_ref[...] = jnp.dot(h.astype(BF16), wr_ref[...], preferred_element_type=F32) + br_ref[...]


def _mix(attn, cv, proj, x2, wa, wc, wo, gpost, g1, gpre, sc, sh, wr, br, seq, col_ga, col_gc):
    n, d = x2.shape
    tm = MIX_TM
    aw = attn.shape[1]
    cw = cv.shape[1]
    ne = wr.shape[1]
    const = lambda shape: pl.BlockSpec(shape, lambda i: (0, 0), pipeline_mode=pl.Buffered(1))
    mod_spec = pl.BlockSpec((None, 1, d), lambda i: ((i * tm) // seq, 0, 0))
    return pl.pallas_call(
        _mix_kernel,
        grid=(n // tm,),
        in_specs=[pl.BlockSpec((tm, aw), lambda i: (i, 0)),
                  pl.BlockSpec((tm, cw), lambda i: (i, 0)),
                  pl.BlockSpec((tm, d), lambda i: (i, col_ga)),
                  pl.BlockSpec((tm, d), lambda i: (i, col_gc)),
                  pl.BlockSpec((tm, d), lambda i: (i, 0)),
                  const((aw, d)), const((cw, d)), const((d, d)),
                  const((1, d)), mod_spec, const((1, d)), mod_spec, mod_spec,
                  const((d, ne)), const((1, ne))],
        out_specs=[pl.BlockSpec((tm, d), lambda i: (i, 0)),
                   pl.BlockSpec((tm, d // 2), lambda i: (i, 0)),
                   pl.BlockSpec((tm, ne), lambda i: (i, 0))],
        out_shape=[jax.ShapeDtypeStruct((n, d), F32),
                   jax.ShapeDtypeStruct((n, d // 2), U32),
                   jax.ShapeDtypeStruct((n, ne), F32)],
        compiler_params=_params(("arbitrary",)),
        name="mix",
    )(attn, cv, proj, proj, x2, wa, wc, wo, gpost, g1, gpre, sc, sh, wr, br)


def _columns_to_lanes(cols, dtype):
    tm = cols[0].shape[0]
    lane = lax.broadcasted_iota(I32, (tm, len(cols)), 1)
    out = jnp.zeros((tm, len(cols)), dtype)
    for k, col in enumerate(cols):
        out = jnp.where(lane == k, col, out)
    return out


def _topk_kernel(lg_ref, idx_ref, p_ref, cnt_ref):
    logits = lg_ref[...]
    tm, ne = logits.shape
    lane = lax.broadcasted_iota(I32, (tm, ne), 1)
    work = logits
    vals, idxs = [], []
    chosen = jnp.zeros((tm, ne), F32)
    for _ in range(TOP_K):
        m = jnp.max(work, axis=-1, keepdims=True)
        idx = jnp.min(jnp.where(work == m, lane, ne), axis=-1, keepdims=True)
        sel = lane == idx
        vals.append(m)
        idxs.append(idx)
        chosen = chosen + sel.astype(F32)
        work = jnp.where(sel, -jnp.inf, work)
    exps = [jnp.exp(v - vals[0]) for v in vals]
    denom = exps[0]
    for e in exps[1:]:
        denom = denom + e
    idx_ref[...] = _columns_to_lanes(idxs, I32)
    p_ref[...] = _columns_to_lanes([e / denom for e in exps], F32)

    @pl.when(pl.program_id(0) == 0)
    def _():
        cnt_ref[...] = jnp.zeros_like(cnt_ref)

    cnt_ref[...] += jnp.sum(chosen, axis=0, keepdims=True)


def _topk(logits):
    n, ne = logits.shape
    tm = TOPK_TM
    return pl.pallas_call(
        _topk_kernel,
        grid=(n // tm,),
        in_specs=[pl.BlockSpec((tm, ne), lambda i: (i, 0))],
        out_specs=[pl.BlockSpec((tm, TOP_K), lambda i: (i, 0)),
                   pl.BlockSpec((tm, TOP_K), lambda i: (i, 0)),
                   pl.BlockSpec((1, ne), lambda i: (0, 0))],
        out_shape=[jax.ShapeDtypeStruct((n, TOP_K), I32),
                   jax.ShapeDtypeStruct((n, TOP_K), F32),
                   jax.ShapeDtypeStruct((1, ne), F32)],
        compiler_params=_params(("arbitrary",)),
        name="topk",
    )(logits)


def _slots_kernel(idx_ref, start_ref, dest_ref, carry_ref):
    @pl.when(pl.program_id(0) == 0)
    def _():
        carry_ref[...] = jnp.zeros_like(carry_ref)

    idx = idx_ref[...]
    tm = idx.shape[0]
    ne = start_ref.shape[1]
    lane = lax.broadcasted_iota(I32, (tm, ne), 1)
    sels = [lane == idx[:, k:k + 1] for k in range(TOP_K)]
    chosen = jnp.zeros((tm, ne), F32)
    for sel in sels:
        chosen = chosen + sel.astype(F32)
    row = lax.broadcasted_iota(I32, (tm, tm), 0)
    col = lax.broadcasted_iota(I32, (tm, tm), 1)
    earlier = (col < row).astype(BF16)
    rank = jnp.dot(earlier, chosen.astype(BF16), preferred_element_type=F32)
    slot = start_ref[...] + carry_ref[...] + rank
    dests = [jnp.sum(jnp.where(sel, slot, 0.0), axis=-1, keepdims=True).astype(I32)
             for sel in sels]
    dest_ref[...] = _columns_to_lanes(dests, I32)
    carry_ref[...] += jnp.sum(chosen, axis=0, keepdims=True)


def _slots(idx, start_f32):
    n = idx.shape[0]
    ne = start_f32.shape[1]
    tm = SLOT_TM
    return pl.pallas_call(
        _slots_kernel,
        grid=(n // tm,),
        in_specs=[pl.BlockSpec((tm, TOP_K), lambda i: (i, 0)),
                  pl.BlockSpec((1, ne), lambda i: (0, 0))],
        out_specs=pl.BlockSpec((tm, TOP_K), lambda i: (i, 0)),
        out_shape=jax.ShapeDtypeStruct((n, TOP_K), I32),
        scratch_shapes=[pltpu.VMEM((1, ne), F32)],
        compiler_params=_params(("arbitrary",)),
        name="slots",
    )(idx, start_f32)


def _scatter_kernel(dest_ref, cnt_ref, start_ref, h_ref, xs_ref, zero_ref, sem, zsem, *, ne):
    tt = h_ref.shape[0]

    def row_copy(r, k):
        return pltpu.make_async_copy(h_ref.at[pl.ds(r, 1)],
                                     xs_ref.at[pl.ds(dest_ref[r * TOP_K + k], 1)], sem)

    def start_rows(r, carry):
        for k in range(TOP_K):
            row_copy(r, k).start()
        return carry

    def wait_rows(r, carry):
        for k in range(TOP_K):
            row_copy(r, k).wait()
        return carry

    lax.fori_loop(0, tt, start_rows, 0)

    @pl.when(pl.program_id(0) == 0)
    def _():
        zero_ref[...] = jnp.zeros_like(zero_ref)

        def pad_copy(s):
            return pltpu.make_async_copy(zero_ref, xs_ref.at[pl.ds(s, 1)], zsem)

        def per_expert(e, carry):
            cnt = cnt_ref[e]
            lo = start_ref[e] + cnt
            hi = start_ref[e] + ((cnt + MOE_TM - 1) // MOE_TM) * MOE_TM

            def start_pad(s, c):
                pad_copy(s).start()
                return c

            def wait_pad(s, c):
                pad_copy(s).wait()
                return c

            lax.fori_loop(lo, hi, start_pad, 0)
            lax.fori_loop(lo, hi, wait_pad, 0)
            return carry

        lax.fori_loop(0, ne, per_expert, 0)

    lax.fori_loop(0, tt, wait_rows, 0)


def _scatter_rows(dest_flat, counts, starts, h_packed, n_rows):
    n, w = h_packed.shape
    ne = counts.shape[0]
    tt = ROW_TT
    smem_vec = pl.BlockSpec(memory_space=pltpu.SMEM)
    return pl.pallas_call(
        functools.partial(_scatter_kernel, ne=ne),
        grid=(n // tt,),
        in_specs=[pl.BlockSpec((tt * TOP_K,), lambda i: (i,), memory_space=pltpu.SMEM),
                  smem_vec, smem_vec,
                  pl.BlockSpec((tt, w), lambda i: (i, 0))],
        out_specs=pl.BlockSpec(memory_space=pl.ANY),
        out_shape=jax.ShapeDtypeStruct((n_rows, w), U32),
        scratch_shapes=[pltpu.VMEM((1, w), U32),
                        pltpu.SemaphoreType.DMA, pltpu.SemaphoreType.DMA],
        compiler_params=_params(("arbitrary",)),
        name="scatter_rows",
    )(dest_flat, counts, starts, h_packed)


def _new_expert(te_ref, t):
    prev = te_ref[jnp.maximum(t - 1, 0)]
    return jnp.logical_or(t == 0, te_ref[t] != prev)


def _gate_up_kernel(te_ref, nu_ref, xs_ref, wg_ref, wu_ref, bg_ref, bu_ref, a_ref,
                    wg_bf, wu_bf):
    t = pl.program_id(1)

    @pl.when(t < nu_ref[0])
    def _():
        @pl.when(_new_expert(te_ref, t))
        def _():
            wg_bf[...] = wg_ref[...].astype(BF16)
            wu_bf[...] = wu_ref[...].astype(BF16)

        lo, hi = _unpack_bf16_pair(xs_ref[...])
        half = lo.shape[1]
        gate = (jnp.dot(lo, wg_bf[:half, :], preferred_element_type=F32)
                + jnp.dot(hi, wg_bf[half:, :], preferred_element_type=F32) + bg_ref[...])
        up = (jnp.dot(lo, wu_bf[:half, :], preferred_element_type=F32)
              + jnp.dot(hi, wu_bf[half:, :], preferred_element_type=F32) + bu_ref[...])
        gate = jnp.minimum(gate, SWIGLU_LIMIT)
        up = jnp.clip(up, -SWIGLU_LIMIT, SWIGLU_LIMIT)
        glu = gate * jax.nn.sigmoid(SWIGLU_ALPHA * gate)
        a_ref[...] = ((up + 1.0) * glu).astype(BF16)


def _gate_up(tile_e, n_used, xs, w_gate_up, b_gate_up3):
    n_rows, w = xs.shape
    ne, d, two_f = w_gate_up.shape
    f = two_f // 2
    tm, tn = MOE_TM, MOE_TN
    nt = n_rows // tm
    nj = f // tn

    def tile(t, nu):
        return jnp.minimum(t, nu[0] - 1)

    grid_spec = pltpu.PrefetchScalarGridSpec(
        num_scalar_prefetch=2,
        grid=(nj, nt),
        in_specs=[
            pl.BlockSpec((tm, w), lambda j, t, te, nu: (tile(t, nu), 0)),
            pl.BlockSpec((None, d, tn), lambda j, t, te, nu: (te[tile(t, nu)], 0, j)),
            pl.BlockSpec((None, d, tn), lambda j, t, te, nu: (te[tile(t, nu)], 0, nj + j)),
            pl.BlockSpec((None, 1, tn), lambda j, t, te, nu: (te[tile(t, nu)], 0, j)),
            pl.BlockSpec((None, 1, tn), lambda j, t, te, nu: (te[tile(t, nu)], 0, nj + j)),
        ],
        out_specs=pl.BlockSpec((tm, tn), lambda j, t, te, nu: (tile(t, nu), j)),
        scratch_shapes=[pltpu.VMEM((d, tn), BF16), pltpu.VMEM((d, tn), BF16)],
    )
    return pl.pallas_call(
        _gate_up_kernel,
        grid_spec=grid_spec,
        out_shape=jax.ShapeDtypeStruct((n_rows, f), BF16),
        compiler_params=_params(("arbitrary", "arbitrary")),
        name="gate_up",
    )(tile_e, n_used, xs, w_gate_up, w_gate_up, b_gate_up3, b_gate_up3)


def _down_kernel(te_ref, nu_ref, a_ref, wd_ref, bd_ref, y_ref, wd_bf):
    t = pl.program_id(1)

    @pl.when(t < nu_ref[0])
    def _():
        @pl.when(_new_expert(te_ref, t))
        def _():
            wd_bf[...] = wd_ref[...].astype(BF16)

        y_ref[...] = jnp.dot(a_ref[...], wd_bf[...], preferred_element_type=F32) + bd_ref[...]


def _down(tile_e, n_used, act, w_down, b_down3):
    n_rows, f = act.shape
    ne, _, d = w_down.shape
    tm, tn = MOE_TM, MOE_TN
    nt = n_rows // tm
    nj = d // tn

    def tile(t, nu):
        return jnp.minimum(t, nu[0] - 1)

    grid_spec = pltpu.PrefetchScalarGridSpec(
        num_scalar_prefetch=2,
        grid=(nj, nt),
        in_specs=[
            pl.BlockSpec((tm, f), lambda j, t, te, nu: (tile(t, nu), 0)),
            pl.BlockSpec((None, f, tn), lambda j, t, te, nu: (te[tile(t, nu)], 0, j)),
            pl.BlockSpec((None, 1, tn), lambda j, t, te, nu: (te[tile(t, nu)], 0, j)),
        ],
        out_specs=pl.BlockSpec((tm, tn), lambda j, t, te, nu: (tile(t, nu), j)),
        scratch_shapes=[pltpu.VMEM((f, tn), BF16)],
    )
    return pl.pallas_call(
        _down_kernel,
        grid_spec=grid_spec,
        out_shape=jax.ShapeDtypeStruct((n_rows, d), F32),
        compiler_params=_params(("arbitrary", "arbitrary")),
        name="down",
    )(tile_e, n_used, act, w_down, b_down3)


def _combine_kernel(dest_ref, p_ref, x1_ref, gpost_ref, g2_ref, y_ref, o_ref, rows, sem):
    tt = x1_ref.shape[0]

    def row_copy(r, k):
        return pltpu.make_async_copy(y_ref.at[pl.ds(dest_ref[r * TOP_K + k], 1)],
                                     rows.at[k, pl.ds(r, 1)], sem)

    def start_rows(r, carry):
        for k in range(TOP_K):
            row_copy(r, k).start()
        return carry

    def wait_rows(r, carry):
        for k in range(TOP_K):
            row_copy(r, k).wait()
        return carry

    lax.fori_loop(0, tt, start_rows, 0)
    lax.fori_loop(0, tt, wait_rows, 0)

    p = p_ref[...]
    y = p[:, 0:1] * rows[0]
    for k in range(1, TOP_K):
        y = y + p[:, k:k + 1] * rows[k]
    o_ref[...] = x1_ref[...] + g2_ref[...] * (_rms(y, RMS_EPS) * gpost_ref[...])


def _combine(dest_flat, probs, x1, gpost, g2, y_rows, seq):
    n, d = x1.shape
    tt = ROW_TT
    return pl.pallas_call(
        _combine_kernel,
        grid=(n // tt,),
        in_specs=[pl.BlockSpec((tt * TOP_K,), lambda i: (i,), memory_space=pltpu.SMEM),
                  pl.BlockSpec((tt, TOP_K), lambda i: (i, 0)),
                  pl.BlockSpec((tt, d), lambda i: (i, 0)),
                  pl.BlockSpec((1, d), lambda i: (0, 0)),
                  pl.BlockSpec((None, 1, d), lambda i: ((i * tt) // seq, 0, 0)),
                  pl.BlockSpec(memory_space=pl.ANY)],
        out_specs=pl.BlockSpec((tt, d), lambda i: (i, 0)),
        out_shape=jax.ShapeDtypeStruct((n, d), F32),
        scratch_shapes=[pltpu.VMEM((TOP_K, tt, d), F32), pltpu.SemaphoreType.DMA],
        compiler_params=_params(("arbitrary",)),
        name="combine",
    )(dest_flat, probs, x1, gpost, g2, y_rows)


def _layer(x2, c_pad, batch, seq, norm_mix_pre, norm_mix_post, norm_ffn_pre, norm_ffn_post,
           w_ada, b_ada, w_in, rel_bias, w_attn_out, w_dw, b_dw, conv_ln_g, conv_ln_b,
           w_conv_out, w_out, w_router, b_router, w_gate_up, b_gate_up, w_down, b_down):
    n, d = x2.shape
    attn_width = w_attn_out.shape[0]
    conv_width = w_conv_out.shape[0]
    taps = w_dw.shape[0]
    max_rel = (rel_bias.shape[1] - 1) // 2
    ne = w_router.shape[1]
    row = lambda v: v.reshape(1, -1)

    mod = _ada(c_pad, w_ada, row(b_ada))[:batch]
    sh1, sc1, g1, sh2, sc2, g2 = [m.reshape(batch, 1, d) for m in jnp.split(mod, 6, axis=-1)]

    n_gate = 2 * d
    w_in_r = jnp.concatenate([w_in[:, -n_gate:], w_in[:, :-n_gate]], axis=1).astype(BF16)
    proj = _inproj(x2, row(norm_mix_pre), sc1, sh1, w_in_r, seq)

    bias = _attn_bias(rel_bias, ATTN_TQ, LEFT_CHUNKS * CHUNK, max_rel)
    attn = _attention(proj, bias, batch, seq, attn_width, n_gate)

    col_a = (n_gate + 3 * attn_width) // conv_width
    w_dw_pad = jnp.pad(w_dw, ((0, CONV_HALO - taps), (0, 0)))
    cv = _conv(proj, w_dw_pad, row(b_dw), row(conv_ln_g), row(conv_ln_b),
               batch, seq, col_a, col_a + 1, conv_width, taps)

    x1, h_packed, logits = _mix(
        attn, cv, proj, x2, w_attn_out.astype(BF16), w_conv_out.astype(BF16),
        w_out.astype(BF16), row(norm_mix_post), g1, row(norm_ffn_pre), sc2, sh2,
        w_router.astype(BF16), row(b_router), seq, 0, 1)

    idx, probs, counts_f = _topk(logits)
    counts = counts_f.reshape(ne).astype(I32)
    padded = ((counts + MOE_TM - 1) // MOE_TM) * MOE_TM
    ends = jnp.cumsum(padded)
    starts = ends - padded
    n_rows = n * TOP_K + ne * MOE_TM
    n_tiles = n_rows // MOE_TM
    tile_e = jnp.minimum(
        jnp.searchsorted(ends, jnp.arange(n_tiles, dtype=I32) * MOE_TM, side="right"),
        ne - 1).astype(I32)
    n_used = (ends[-1:] // MOE_TM).astype(I32)

    dest = _slots(idx, starts.astype(F32).reshape(1, ne))
    dest_flat = dest.reshape(n * TOP_K)
    xs = _scatter_rows(dest_flat, counts, starts, h_packed, n_rows)
    act = _gate_up(tile_e, n_used, xs, w_gate_up, b_gate_up.reshape(ne, 1, -1))
    y_rows = _down(tile_e, n_used, act, w_down, b_down.reshape(ne, 1, -1))
    return _combine(dest_flat, probs, x1, row(norm_ffn_post), g2, y_rows, seq)


def kernel(x, c, norm_mix_pre, norm_mix_post, norm_ffn_pre, norm_ffn_post, w_ada, b_ada, w_in,
           rel_bias, w_attn_out, w_dw, b_dw, conv_ln_g, conv_ln_b, w_conv_out, w_out, w_router,
           b_router, w_gate_up, b_gate_up, w_down, b_down):
    batch, seq, d = x.shape
    x2 = x.reshape(batch * seq, d)
    c_pad = jnp.pad(c, ((0, SUBLANES - batch % SUBLANES), (0, 0))) if batch % SUBLANES else c
    for l in range(w_ada.shape[0]):
        x2 = _layer(x2, c_pad, batch, seq, norm_mix_pre[l], norm_mix_post[l], norm_ffn_pre[l],
                    norm_ffn_post[l], w_ada[l], b_ada[l], w_in[l], rel_bias[l], w_attn_out[l],
                    w_dw[l], b_dw[l], conv_ln_g[l], conv_ln_b[l], w_conv_out[l], w_out[l],
                    w_router[l], b_router[l], w_gate_up[l], b_gate_up[l], w_down[l], b_down[l])
    return x2.reshape(batch, seq, d)
```

```python
import functools

import numpy as np

import jax
import jax.numpy as jnp
from jax import lax
from jax.experimental import pallas as pl
from jax.experimental.pallas import tpu as pltpu

F32 = jnp.float32
BF16 = jnp.bfloat16
I32 = jnp.int32
U32 = jnp.uint32

CHUNK = 64
LEFT_CHUNKS = 8
HEAD_DIM = 64
TOP_K = 4
SWIGLU_LIMIT = 7.0
SWIGLU_ALPHA = 1.702
RMS_EPS = 1e-6
LN_EPS = 1e-5
MASK_VALUE = -1e30

LANES = 128
SUBLANES = 8
VMEM_LIMIT_BYTES = 56 * 1024 * 1024

ADA_TN = 1024
INPROJ_TM = 1024
INPROJ_TN = 1024
ATTN_TQ = 256
CONV_TS = 128
CONV_CG = 256
CONV_HALO = 32
MIX_TM = 256
TOPK_TM = 1024
SLOT_TM = 512
MOE_TM = 256
GATE_UP_TN = 1024
ROW_TT = 256


def _params(sem, vmem=VMEM_LIMIT_BYTES):
    return pltpu.CompilerParams(dimension_semantics=sem, vmem_limit_bytes=vmem)


def _rms(x, eps):
    return x * lax.rsqrt(jnp.mean(x * x, axis=-1, keepdims=True) + eps)


def _pack_bf16_pair(lo, hi):
    lo_bits = lax.bitcast_convert_type(lo.astype(BF16).astype(F32), U32)
    hi_bits = lax.bitcast_convert_type(hi.astype(BF16).astype(F32), U32)
    return (lo_bits >> 16) | (hi_bits & jnp.uint32(0xFFFF0000))


def _unpack_pair_f32(packed):
    lo = lax.bitcast_convert_type(packed << 16, F32)
    hi = lax.bitcast_convert_type(packed & jnp.uint32(0xFFFF0000), F32)
    return lo, hi


def _store_row_tiles(ref, packed):
    rows = packed.shape[0]
    for s in range(SUBLANES):
        ref[pl.ds(s, rows, stride=SUBLANES), :] = packed[:, s * LANES:(s + 1) * LANES]


def _load_row_tiles(ref):
    rows = ref.shape[0] // SUBLANES
    return jnp.concatenate(
        [ref[pl.ds(s, rows, stride=SUBLANES), :] for s in range(SUBLANES)], axis=1)


def _row_tile(ref, row):
    return ref.at[pl.ds(pl.multiple_of(row * SUBLANES, SUBLANES), SUBLANES)]


def _ada_kernel(c_ref, w_ref, b_ref, o_ref):
    c = c_ref[...]
    s = c * jax.nn.sigmoid(c)
    o_ref[...] = jnp.dot(s.astype(BF16), w_ref[...].astype(BF16),
                         preferred_element_type=F32) + b_ref[...]


def _ada(c_pad, w_ada, b_ada):
    rows, d = c_pad.shape
    n_out = w_ada.shape[1]
    return pl.pallas_call(
        _ada_kernel,
        grid=(n_out // ADA_TN,),
        in_specs=[pl.BlockSpec((rows, d), lambda j: (0, 0)),
                  pl.BlockSpec((d, ADA_TN), lambda j: (0, j)),
                  pl.BlockSpec((1, ADA_TN), lambda j: (0, j))],
        out_specs=pl.BlockSpec((rows, ADA_TN), lambda j: (0, j)),
        out_shape=jax.ShapeDtypeStruct((rows, n_out), F32),
        compiler_params=_params(("arbitrary",)),
        name="ada",
    )(c_pad, w_ada, b_ada)


def _inproj_kernel(x_ref, g_ref, sc_ref, sh_ref, w_ref, o_ref, h_ref):
    @pl.when(pl.program_id(1) == 0)
    def _():
        y = _rms(x_ref[...], RMS_EPS) * g_ref[...]
        h_ref[...] = (y * (1.0 + sc_ref[...]) + sh_ref[...]).astype(BF16)

    o_ref[...] = jnp.dot(h_ref[...], w_ref[...], preferred_element_type=F32).astype(BF16)


def _inproj(x2, g, sc, sh, w_bf16, seq):
    n, d = x2.shape
    width = w_bf16.shape[1]
    tm, tn = INPROJ_TM, INPROJ_TN
    mod_spec = pl.BlockSpec((None, 1, d), lambda i, j: ((i * tm) // seq, 0, 0))
    return pl.pallas_call(
        _inproj_kernel,
        grid=(n // tm, width // tn),
        in_specs=[pl.BlockSpec((tm, d), lambda i, j: (i, 0)),
                  pl.BlockSpec((1, d), lambda i, j: (0, 0)),
                  mod_spec, mod_spec,
                  pl.BlockSpec((d, tn), lambda i, j: (0, j))],
        out_specs=pl.BlockSpec((tm, tn), lambda i, j: (i, j)),
        out_shape=jax.ShapeDtypeStruct((n, width), BF16),
        scratch_shapes=[pltpu.VMEM((tm, d), BF16)],
        compiler_params=_params(("arbitrary", "arbitrary")),
        name="inproj",
    )(x2, g, sc, sh, w_bf16)


def _attn_kernel(q_ref, k0_ref, k1_ref, k2_ref, v0_ref, v1_ref, v2_ref, g_ref, mask_ref,
                 o_ref, bias_ref):
    heads = g_ref.shape[0]
    tq = q_ref.shape[0]
    win = mask_ref.shape[-1]

    @pl.when(jnp.logical_and(pl.program_id(0) == 0, pl.program_id(1) == 0))
    def _():
        for h in range(heads):
            rows = jnp.broadcast_to(g_ref[h], (tq, g_ref.shape[-1]))
            bias_ref[h] = pltpu.roll(rows, 0, 1, stride=1, stride_axis=0)[:, :win]

    scale = HEAD_DIM ** -0.5
    per_vreg = LANES // HEAD_DIM
    for hp in range(heads // per_vreg):
        cols = slice(hp * LANES, (hp + 1) * LANES)
        q = q_ref[:, cols]
        k = jnp.concatenate([k0_ref[:, cols], k1_ref[:, cols], k2_ref[:, cols]], axis=0)
        v = jnp.concatenate([v0_ref[:, cols], v1_ref[:, cols], v2_ref[:, cols]], axis=0)
        outs = []
        for hh in range(per_vreg):
            sl = slice(hh * HEAD_DIM, (hh + 1) * HEAD_DIM)
            s = lax.dot_general(q[:, sl], k[:, sl], (((1,), (1,)), ((), ())),
                                preferred_element_type=F32)
            s = s * scale + bias_ref[hp * per_vreg + hh] + mask_ref[...]
            m = jnp.max(s, axis=-1, keepdims=True)
            p = jnp.exp(s - m)
            l = jnp.sum(p, axis=-1, keepdims=True)
            o = jnp.dot(p.astype(BF16), v[:, sl], preferred_element_type=F32)
            outs.append(o / l)
        o_ref[:, cols] = jnp.concatenate(outs, axis=1).astype(BF16)


def _attn_tables(rel_bias, tq, left, max_rel):
    win = left + tq
    length = pl.next_power_of_2(win + tq)
    u = np.arange(length)
    signed = np.where(u < win, u, u - length)
    idx = np.clip(left - signed, -max_rel, max_rel) + max_rel
    g = rel_bias[:, idx].astype(F32).reshape(rel_bias.shape[0], 1, length)

    r = np.arange(tq)[:, None]
    c = np.arange(win)[None, :]
    band = (c // CHUNK >= r // CHUNK) & (c // CHUNK <= r // CHUNK + LEFT_CHUNKS)
    masks = [np.where(band & (c >= left - var * tq), 0.0, MASK_VALUE)
             for var in range(left // tq + 1)]
    return g, jnp.asarray(np.stack(masks), F32)


def _attention(proj, g, masks, batch, seq, attn_width, q_off):
    n = proj.shape[0]
    tq = ATTN_TQ
    nq = seq // tq
    heads = g.shape[0]
    win = masks.shape[-1]
    nvar = masks.shape[0]
    nback = nvar - 1
    qcol = q_off // attn_width
    blk = (tq, attn_width)

    def kv_spec(col, back):
        return pl.BlockSpec(blk, lambda b, qi: (b * nq + jnp.maximum(qi - back, 0), col))

    in_specs = [pl.BlockSpec(blk, lambda b, qi: (b * nq + qi, qcol))]
    in_specs += [kv_spec(qcol + 1, back) for back in range(nback, -1, -1)]
    in_specs += [kv_spec(qcol + 2, back) for back in range(nback, -1, -1)]
    in_specs += [pl.BlockSpec(g.shape, lambda b, qi: (0, 0, 0)),
                 pl.BlockSpec((None, tq, win), lambda b, qi: (jnp.minimum(qi, nvar - 1), 0, 0))]
    return pl.pallas_call(
        _attn_kernel,
        grid=(batch, nq),
        in_specs=in_specs,
        out_specs=pl.BlockSpec(blk, lambda b, qi: (b * nq + qi, 0)),
        out_shape=jax.ShapeDtypeStruct((n, attn_width), BF16),
        scratch_shapes=[pltpu.VMEM((heads, tq, win), F32)],
        compiler_params=_params(("arbitrary", "arbitrary")),
        name="attn",
    )(proj, *([proj] * (2 * (nback + 1))), g, masks)


def _conv_kernel(ca_ref, cb_ref, w_ref, b_ref, g_ref, beta_ref, o_ref, ubuf, ush, cbuf, *, taps):
    ts = ca_ref.shape[0]
    width = ca_ref.shape[1]
    halo = CONV_HALO
    first = pl.program_id(1) == 0

    @pl.when(first)
    def _():
        ubuf[0:halo, :] = jnp.zeros((halo, width), F32)

    @pl.when(jnp.logical_not(first))
    def _():
        ubuf[0:halo, :] = ubuf[ts:ts + halo, :]

    ubuf[halo:halo + ts, :] = (ca_ref[...].astype(F32)
                               * jax.nn.sigmoid(cb_ref[...].astype(F32)))
    span = ush.shape[1]
    for s in range(1, SUBLANES):
        ush[s - 1] = ubuf[s:s + span, :]
    off = halo - (taps - 1)
    for cg in range(width // CONV_CG):
        cs = slice(cg * CONV_CG, (cg + 1) * CONV_CG)
        acc = jnp.zeros((ts, CONV_CG), F32)
        for j in range(taps):
            base, s = divmod(off + j, SUBLANES)
            base *= SUBLANES
            src = ubuf[base:base + ts, cs] if s == 0 else ush[s - 1, base:base + ts, cs]
            acc = acc + w_ref[j:j + 1, cs] * src
        cbuf[:, cs] = acc + b_ref[:, cs]
    y = cbuf[...]
    mu = jnp.mean(y, axis=-1, keepdims=True)
    yc = y - mu
    var = jnp.mean(yc * yc, axis=-1, keepdims=True)
    yn = yc * lax.rsqrt(var + LN_EPS) * g_ref[...] + beta_ref[...]
    o_ref[...] = (yn * jax.nn.sigmoid(yn)).astype(BF16)


def _conv(proj, w_dw_pad, b_dw, ln_g, ln_b, batch, seq, col_a, col_b, width, taps):
    n = proj.shape[0]
    ts = CONV_TS
    ns = seq // ts
    vec = pl.BlockSpec((1, width), lambda b, i: (0, 0))
    return pl.pallas_call(
        functools.partial(_conv_kernel, taps=taps),
        grid=(batch, ns),
        in_specs=[pl.BlockSpec((ts, width), lambda b, i: (b * ns + i, col_a)),
                  pl.BlockSpec((ts, width), lambda b, i: (b * ns + i, col_b)),
                  pl.BlockSpec(w_dw_pad.shape, lambda b, i: (0, 0)),
                  vec, vec, vec],
        out_specs=pl.BlockSpec((ts, width), lambda b, i: (b * ns + i, 0)),
        out_shape=jax.ShapeDtypeStruct((n, width), BF16),
        scratch_shapes=[pltpu.VMEM((CONV_HALO + ts, width), F32),
                        pltpu.VMEM((SUBLANES - 1, CONV_HALO + ts - SUBLANES, width), F32),
                        pltpu.VMEM((ts, width), F32)],
        compiler_params=_params(("arbitrary", "arbitrary")),
        name="conv",
    )(proj, proj, w_dw_pad, b_dw, ln_g, ln_b)


def _mix_kernel(attn_ref, cv_ref, ga0_ref, ga1_ref, gc0_ref, gc1_ref, x_ref, wa_ref, wc_ref,
                wo_ref, gpost_ref, g1_ref, gpre_ref, sc_ref, sh_ref, wr_ref, br_ref,
                x1_ref, hp_ref, lg_ref):
    ya = jnp.dot(attn_ref[...], wa_ref[...], preferred_element_type=F32)
    yc = jnp.dot(cv_ref[...], wc_ref[...], preferred_element_type=F32)
    ga = jnp.concatenate([ga0_ref[...], ga1_ref[...]], axis=1).astype(F32)
    gc = jnp.concatenate([gc0_ref[...], gc1_ref[...]], axis=1).astype(F32)
    merged = jax.nn.sigmoid(ga) * ya + jax.nn.sigmoid(gc) * yc
    y = jnp.dot(merged.astype(BF16), wo_ref[...], preferred_element_type=F32)
    x1 = x_ref[...] + g1_ref[...] * (_rms(y, RMS_EPS) * gpost_ref[...])
    x1_ref[...] = x1
    h = _rms(x1, RMS_EPS) * gpre_ref[...] * (1.0 + sc_ref[...]) + sh_ref[...]
    half = h.shape[1] // 2
    _store_row_tiles(hp_ref, _pack_bf16_pair(h[:, :half], h[:, half:]))
    lg_ref[...] = jnp.dot(h.astype(BF16), wr_ref[...], preferred_element_type=F32) + br_ref[...]


def _mix(attn, cv, proj, x2, wa, wc, wo, gpost, g1, gpre, sc, sh, wr, br, seq, col_g):
    n, d = x2.shape
    tm = MIX_TM
    aw = attn.shape[1]
    cw = cv.shape[1]
    ne = wr.shape[1]
    gw = d // 2
    const = lambda shape: pl.BlockSpec(shape, lambda i: (0, 0), pipeline_mode=pl.Buffered(1))
    mod_spec = pl.BlockSpec((None, 1, d), lambda i: ((i * tm) // seq, 0, 0))
    gate_spec = lambda col: pl.BlockSpec((tm, gw), lambda i: (i, col))
    return pl.pallas_call(
        _mix_kernel,
        grid=(n // tm,),
        in_specs=[pl.BlockSpec((tm, aw), lambda i: (i, 0)),
                  pl.BlockSpec((tm, cw), lambda i: (i, 0)),
                  gate_spec(col_g), gate_spec(col_g + 1),
                  gate_spec(col_g + 2), gate_spec(col_g + 3),
                  pl.BlockSpec((tm, d), lambda i: (i, 0)),
                  const((aw, d)), const((cw, d)), const((d, d)),
                  const((1, d)), mod_spec, const((1, d)), mod_spec, mod_spec,
                  const((d, ne)), const((1, ne))],
        out_specs=[pl.BlockSpec((tm, d), lambda i: (i, 0)),
                   pl.BlockSpec((tm * SUBLANES, LANES), lambda i: (i, 0)),
                   pl.BlockSpec((tm, ne), lambda i: (i, 0))],
        out_shape=[jax.ShapeDtypeStruct((n, d), F32),
                   jax.ShapeDtypeStruct((n * SUBLANES, LANES), U32),
                   jax.ShapeDtypeStruct((n, ne), F32)],
        compiler_params=_params(("arbitrary",)),
        name="mix",
    )(attn, cv, proj, proj, proj, proj, x2, wa, wc, wo, gpost, g1, gpre, sc, sh, wr, br)


def _columns_to_lanes(cols, dtype):
    tm = cols[0].shape[0]
    lane = lax.broadcasted_iota(I32, (tm, len(cols)), 1)
    out = jnp.zeros((tm, len(cols)), dtype)
    for k, col in enumerate(cols):
        out = jnp.where(lane == k, col, out)
    return out


def _topk_kernel(lg_ref, idx_ref, p_ref, cnt_ref):
    logits = lg_ref[...]
    tm, ne = logits.shape
    lane = lax.broadcasted_iota(I32, (tm, ne), 1)
    work = logits
    vals, idxs = [], []
    chosen = jnp.zeros((tm, ne), F32)
    for _ in range(TOP_K):
        m = jnp.max(work, axis=-1, keepdims=True)
        idx = jnp.min(jnp.where(work == m, lane, ne), axis=-1, keepdims=True)
        sel = lane == idx
        vals.append(m)
        idxs.append(idx)
        chosen = chosen + sel.astype(F32)
        work = jnp.where(sel, -jnp.inf, work)
    exps = [jnp.exp(v - vals[0]) for v in vals]
    denom = exps[0]
    for e in exps[1:]:
        denom = denom + e
    idx_ref[...] = _columns_to_lanes(idxs, I32)
    p_ref[...] = _columns_to_lanes([e / denom for e in exps], F32)

    @pl.when(pl.program_id(0) == 0)
    def _():
        cnt_ref[...] = jnp.zeros_like(cnt_ref)

    cnt_ref[...] += jnp.sum(chosen, axis=0, keepdims=True)


def _topk(logits):
    n, ne = logits.shape
    tm = TOPK_TM
    return pl.pallas_call(
        _topk_kernel,
        grid=(n // tm,),
        in_specs=[pl.BlockSpec((tm, ne), lambda i: (i, 0))],
        out_specs=[pl.BlockSpec((tm, TOP_K), lambda i: (i, 0)),
                   pl.BlockSpec((tm, TOP_K), lambda i: (i, 0)),
                   pl.BlockSpec((1, ne), lambda i: (0, 0))],
        out_shape=[jax.ShapeDtypeStruct((n, TOP_K), I32),
                   jax.ShapeDtypeStruct((n, TOP_K), F32),
                   jax.ShapeDtypeStruct((1, ne), F32)],
        compiler_params=_params(("arbitrary",)),
        name="topk",
    )(logits)


def _slots_kernel(idx_ref, start_ref, dest_ref, carry_ref):
    @pl.when(pl.program_id(0) == 0)
    def _():
        carry_ref[...] = jnp.zeros_like(carry_ref)

    idx = idx_ref[...]
    tm = idx.shape[0]
    ne = start_ref.shape[1]
    lane = lax.broadcasted_iota(I32, (tm, ne), 1)
    sels = [lane == idx[:, k:k + 1] for k in range(TOP_K)]
    chosen = jnp.zeros((tm, ne), F32)
    for sel in sels:
        chosen = chosen + sel.astype(F32)
    row = lax.broadcasted_iota(I32, (tm, tm), 0)
    col = lax.broadcasted_iota(I32, (tm, tm), 1)
    earlier = (col < row).astype(BF16)
    rank = jnp.dot(earlier, chosen.astype(BF16), preferred_element_type=F32)
    slot = start_ref[...] + carry_ref[...] + rank
    dests = [jnp.sum(jnp.where(sel, slot, 0.0), axis=-1, keepdims=True).astype(I32)
             for sel in sels]
    dest_ref[...] = _columns_to_lanes(dests, I32)
    carry_ref[...] += jnp.sum(chosen, axis=0, keepdims=True)


def _slots(idx, start_f32):
    n = idx.shape[0]
    ne = start_f32.shape[1]
    tm = SLOT_TM
    return pl.pallas_call(
        _slots_kernel,
        grid=(n // tm,),
        in_specs=[pl.BlockSpec((tm, TOP_K), lambda i: (i, 0)),
                  pl.BlockSpec((1, ne), lambda i: (0, 0))],
        out_specs=pl.BlockSpec((tm, TOP_K), lambda i: (i, 0)),
        out_shape=jax.ShapeDtypeStruct((n, TOP_K), I32),
        scratch_shapes=[pltpu.VMEM((1, ne), F32)],
        compiler_params=_params(("arbitrary",)),
        name="slots",
    )(idx, start_f32)


def _scatter_kernel(dest_ref, cnt_ref, start_ref, nu_ref, h_ref, xs_ref, zero_ref, sem, zsem,
                    *, ne, n_tiles):
    tt = h_ref.shape[0] // SUBLANES

    def row_copy(r, k):
        return pltpu.make_async_copy(_row_tile(h_ref, r),
                                     _row_tile(xs_ref, dest_ref[r * TOP_K + k]), sem)

    def start_rows(r, carry):
        for k in range(TOP_K):
            row_copy(r, k).start()
        return carry

    def wait_rows(r, carry):
        for k in range(TOP_K):
            row_copy(r, k).wait()
        return carry

    lax.fori_loop(0, tt, start_rows, 0)

    @pl.when(pl.program_id(0) == 0)
    def _():
        zero_ref[...] = jnp.zeros_like(zero_ref)

        tile_rows = MOE_TM * SUBLANES

        def pad_copy(s):
            return pltpu.make_async_copy(zero_ref.at[pl.ds(0, SUBLANES)],
                                         _row_tile(xs_ref, s), zsem)

        def tile_copy(t):
            dst = xs_ref.at[pl.ds(pl.multiple_of(t * tile_rows, tile_rows), tile_rows)]
            return pltpu.make_async_copy(zero_ref, dst, zsem)

        def loop(lo, hi, fn):
            def body(i, c):
                fn(i)
                return c
            lax.fori_loop(lo, hi, body, 0)

        def per_expert(e, carry):
            cnt = cnt_ref[e]
            lo = start_ref[e] + cnt
            hi = start_ref[e] + ((cnt + MOE_TM - 1) // MOE_TM) * MOE_TM
            loop(lo, hi, lambda s: pad_copy(s).start())
            loop(lo, hi, lambda s: pad_copy(s).wait())
            return carry

        lax.fori_loop(0, ne, per_expert, 0)
        loop(nu_ref[0], n_tiles, lambda t: tile_copy(t).start())
        loop(nu_ref[0], n_tiles, lambda t: tile_copy(t).wait())

    lax.fori_loop(0, tt, wait_rows, 0)


def _scatter_rows(dest_flat, counts, starts, n_used, h_tiles, n_rows):
    ne = counts.shape[0]
    tt = ROW_TT
    n = h_tiles.shape[0] // SUBLANES
    smem = pl.BlockSpec(memory_space=pltpu.SMEM)
    return pl.pallas_call(
        functools.partial(_scatter_kernel, ne=ne, n_tiles=n_rows // MOE_TM),
        grid=(n // tt,),
        in_specs=[pl.BlockSpec((tt * TOP_K,), lambda i: (i,), memory_space=pltpu.SMEM),
                  smem, smem, smem,
                  pl.BlockSpec((tt * SUBLANES, LANES), lambda i: (i, 0))],
        out_specs=pl.BlockSpec(memory_space=pl.ANY),
        out_shape=jax.ShapeDtypeStruct((n_rows * SUBLANES, LANES), U32),
        scratch_shapes=[pltpu.VMEM((MOE_TM * SUBLANES, LANES), U32),
                        pltpu.SemaphoreType.DMA, pltpu.SemaphoreType.DMA],
        compiler_params=_params(("arbitrary",)),
        name="scatter_rows",
    )(dest_flat, counts, starts, n_used, h_tiles)


def _new_expert(te_ref, t):
    prev = te_ref[jnp.maximum(t - 1, 0)]
    return jnp.logical_or(t == 0, te_ref[t] != prev)


def _gate_up_kernel(te_ref, nu_ref, xs_ref, wg_ref, wu_ref, bg_ref, bu_ref, a_ref,
                    wg_bf, wu_bf):
    t = pl.program_id(1)
    used = t < nu_ref[0]

    @pl.when(used)
    def _():
        @pl.when(_new_expert(te_ref, t))
        def _():
            wg_bf[...] = wg_ref[...].astype(BF16)
            wu_bf[...] = wu_ref[...].astype(BF16)

        lo, hi = _unpack_pair_f32(_load_row_tiles(xs_ref))
        lo = lo.astype(BF16)
        hi = hi.astype(BF16)
        half = lo.shape[1]
        gate = (jnp.dot(lo, wg_bf[:half, :], preferred_element_type=F32)
                + jnp.dot(hi, wg_bf[half:, :], preferred_element_type=F32) + bg_ref[...])
        up = (jnp.dot(lo, wu_bf[:half, :], preferred_element_type=F32)
              + jnp.dot(hi, wu_bf[half:, :], preferred_element_type=F32) + bu_ref[...])
        gate = jnp.minimum(gate, SWIGLU_LIMIT)
        up = jnp.clip(up, -SWIGLU_LIMIT, SWIGLU_LIMIT)
        glu = gate * jax.nn.sigmoid(SWIGLU_ALPHA * gate)
        a_ref[...] = ((up + 1.0) * glu).astype(BF16)

    @pl.when(jnp.logical_not(used))
    def _():
        a_ref[...] = jnp.zeros_like(a_ref)


def _gate_up(tile_e, n_used, xs, w_gate_up, b_gate_up3):
    ne, d, two_f = w_gate_up.shape
    f = two_f // 2
    tm, tn = MOE_TM, GATE_UP_TN
    nt = xs.shape[0] // (tm * SUBLANES)
    nj = f // tn

    def tile(t, nu):
        return jnp.minimum(t, nu[0] - 1)

    grid_spec = pltpu.PrefetchScalarGridSpec(
        num_scalar_prefetch=2,
        grid=(nj, nt),
        in_specs=[
            pl.BlockSpec((tm * SUBLANES, LANES), lambda j, t, te, nu: (tile(t, nu), 0)),
            pl.BlockSpec((None, d, tn), lambda j, t, te, nu: (te[tile(t, nu)], 0, j)),
            pl.BlockSpec((None, d, tn), lambda j, t, te, nu: (te[tile(t, nu)], 0, nj + j)),
            pl.BlockSpec((None, 1, tn), lambda j, t, te, nu: (te[tile(t, nu)], 0, j)),
            pl.BlockSpec((None, 1, tn), lambda j, t, te, nu: (te[tile(t, nu)], 0, nj + j)),
        ],
        out_specs=pl.BlockSpec((tm, tn), lambda j, t, te, nu: (t, j)),
        scratch_shapes=[pltpu.VMEM((d, tn), BF16), pltpu.VMEM((d, tn), BF16)],
    )
    return pl.pallas_call(
        _gate_up_kernel,
        grid_spec=grid_spec,
        out_shape=jax.ShapeDtypeStruct((nt * tm, f), BF16),
        compiler_params=_params(("arbitrary", "arbitrary")),
        name="gate_up",
    )(tile_e, n_used, xs, w_gate_up, w_gate_up, b_gate_up3, b_gate_up3)


def _down_kernel(te_ref, nu_ref, a_ref, wd_ref, bd_ref, y_ref, wd_bf):
    t = pl.program_id(0)
    used = t < nu_ref[0]

    @pl.when(used)
    def _():
        @pl.when(_new_expert(te_ref, t))
        def _():
            wd_bf[...] = wd_ref[...].astype(BF16)

        y = jnp.dot(a_ref[...], wd_bf[...], preferred_element_type=F32) + bd_ref[...]
        half = y.shape[1] // 2
        _store_row_tiles(y_ref, _pack_bf16_pair(y[:, :half], y[:, half:]))

    @pl.when(jnp.logical_not(used))
    def _():
        y_ref[...] = jnp.zeros_like(y_ref)


def _down(tile_e, n_used, act, w_down, b_down3):
    n_rows, f = act.shape
    ne, _, d = w_down.shape
    tm = MOE_TM
    nt = n_rows // tm

    def tile(t, nu):
        return jnp.minimum(t, nu[0] - 1)

    grid_spec = pltpu.PrefetchScalarGridSpec(
        num_scalar_prefetch=2,
        grid=(nt,),
        in_specs=[
            pl.BlockSpec((tm, f), lambda t, te, nu: (tile(t, nu), 0)),
            pl.BlockSpec((None, f, d), lambda t, te, nu: (te[tile(t, nu)], 0, 0)),
            pl.BlockSpec((None, 1, d), lambda t, te, nu: (te[tile(t, nu)], 0, 0)),
        ],
        out_specs=pl.BlockSpec((tm * SUBLANES, LANES), lambda t, te, nu: (t, 0)),
        scratch_shapes=[pltpu.VMEM((f, d), BF16)],
    )
    return pl.pallas_call(
        _down_kernel,
        grid_spec=grid_spec,
        out_shape=jax.ShapeDtypeStruct((n_rows * SUBLANES, LANES), U32),
        compiler_params=_params(("arbitrary",)),
        name="down",
    )(tile_e, n_used, act, w_down, b_down3)


def _combine_kernel(dest_ref, p_ref, x1_ref, gpost_ref, g2_ref, y_ref, o_ref,
                    r0, r1, r2, r3, sem):
    tt = x1_ref.shape[0]
    bufs = (r0, r1, r2, r3)

    def row_copy(r, k):
        return pltpu.make_async_copy(_row_tile(y_ref, dest_ref[r * TOP_K + k]),
                                     _row_tile(bufs[k], r), sem)

    def start_rows(r, carry):
        for k in range(TOP_K):
            row_copy(r, k).start()
        return carry

    def wait_rows(r, carry):
        for k in range(TOP_K):
            row_copy(r, k).wait()
        return carry

    lax.fori_loop(0, tt, start_rows, 0)
    lax.fori_loop(0, tt, wait_rows, 0)

    p = p_ref[...]
    y_lo = None
    for k in range(TOP_K):
        lo, hi = _unpack_pair_f32(_load_row_tiles(bufs[k]))
        w = p[:, k:k + 1]
        y_lo = w * lo if y_lo is None else y_lo + w * lo
        y_hi = w * hi if k == 0 else y_hi + w * hi
    y = jnp.concatenate([y_lo, y_hi], axis=1)
    o_ref[...] = x1_ref[...] + g2_ref[...] * (_rms(y, RMS_EPS) * gpost_ref[...])


def _combine(dest_flat, probs, x1, gpost, g2, y_tiles, seq):
    n, d = x1.shape
    tt = ROW_TT
    return pl.pallas_call(
        _combine_kernel,
        grid=(n // tt,),
        in_specs=[pl.BlockSpec((tt * TOP_K,), lambda i: (i,), memory_space=pltpu.SMEM),
                  pl.BlockSpec((tt, TOP_K), lambda i: (i, 0)),
                  pl.BlockSpec((tt, d), lambda i: (i, 0)),
                  pl.BlockSpec((1, d), lambda i: (0, 0)),
                  pl.BlockSpec((None, 1, d), lambda i: ((i * tt) // seq, 0, 0)),
                  pl.BlockSpec(memory_space=pl.ANY)],
        out_specs=pl.BlockSpec((tt, d), lambda i: (i, 0)),
        out_shape=jax.ShapeDtypeStruct((n, d), F32),
        scratch_shapes=[pltpu.VMEM((tt * SUBLANES, LANES), U32)] * TOP_K
                       + [pltpu.SemaphoreType.DMA],
        compiler_params=_params(("arbitrary",)),
        name="combine",
    )(dest_flat, probs, x1, gpost, g2, y_tiles)


def _layer(x2, c_pad, batch, seq, norm_mix_pre, norm_mix_post, norm_ffn_pre, norm_ffn_post,
           w_ada, b_ada, w_in, rel_bias, w_attn_out, w_dw, b_dw, conv_ln_g, conv_ln_b,
           w_conv_out, w_out, w_router, b_router, w_gate_up, b_gate_up, w_down, b_down):
    n, d = x2.shape
    attn_width = w_attn_out.shape[0]
    conv_width = w_conv_out.shape[0]
    taps = w_dw.shape[0]
    max_rel = (rel_bias.shape[1] - 1) // 2
    ne = w_router.shape[1]
    assert d // 2 == SUBLANES * LANES, "row tiles assume 2 * 8 * 128 features per token"
    row = lambda v: v.reshape(1, -1)

    mod = _ada(c_pad, w_ada, row(b_ada))[:batch]
    sh1, sc1, g1, sh2, sc2, g2 = [m.reshape(batch, 1, d) for m in jnp.split(mod, 6, axis=-1)]

    proj = _inproj(x2, row(norm_mix_pre), sc1, sh1, w_in.astype(BF16), seq)

    g, masks = _attn_tables(rel_bias, ATTN_TQ, LEFT_CHUNKS * CHUNK, max_rel)
    attn = _attention(proj, g, masks, batch, seq, attn_width, 0)

    col_a = 3 * attn_width // conv_width
    w_dw_pad = jnp.pad(w_dw, ((0, CONV_HALO - taps), (0, 0)))
    cv = _conv(proj, w_dw_pad, row(b_dw), row(conv_ln_g), row(conv_ln_b),
               batch, seq, col_a, col_a + 1, conv_width, taps)

    col_g = (3 * attn_width + 2 * conv_width) // (d // 2)
    x1, h_tiles, logits = _mix(
        attn, cv, proj, x2, w_attn_out.astype(BF16), w_conv_out.astype(BF16),
        w_out.astype(BF16), row(norm_mix_post), g1, row(norm_ffn_pre), sc2, sh2,
        w_router.astype(BF16), row(b_router), seq, col_g)

    idx, probs, counts_f = _topk(logits)
    counts = counts_f.reshape(ne).astype(I32)
    padded = ((counts + MOE_TM - 1) // MOE_TM) * MOE_TM
    ends = jnp.cumsum(padded)
    starts = ends - padded
    n_rows = n * TOP_K + ne * MOE_TM
    n_tiles = n_rows // MOE_TM
    tile_start = jnp.arange(n_tiles, dtype=I32) * MOE_TM
    tile_e = jnp.minimum(jnp.sum((tile_start[:, None] >= ends[None, :]).astype(I32), axis=1),
                         ne - 1)
    n_used = (ends[-1:] // MOE_TM).astype(I32)

    dest = _slots(idx, starts.astype(F32).reshape(1, ne))
    dest_flat = dest.reshape(n * TOP_K)
    xs = _scatter_rows(dest_flat, counts, starts, n_used, h_tiles, n_rows)
    act = _gate_up(tile_e, n_used, xs, w_gate_up, b_gate_up.reshape(ne, 1, -1))
    y_tiles = _down(tile_e, n_used, act, w_down, b_down.reshape(ne, 1, -1))
    return _combine(dest_flat, probs, x1, row(norm_ffn_post), g2, y_tiles, seq)


def kernel(x, c, norm_mix_pre, norm_mix_post, norm_ffn_pre, norm_ffn_post, w_ada, b_ada, w_in,
           rel_bias, w_attn_out, w_dw, b_dw, conv_ln_g, conv_ln_b, w_conv_out, w_out, w_router,
           b_router, w_gate_up, b_gate_up, w_down, b_down):
    batch, seq, d = x.shape
    x2 = x.reshape(batch * seq, d)
    c_pad = jnp.pad(c, ((0, SUBLANES - batch % SUBLANES), (0, 0))) if batch % SUBLANES else c
    for l in range(w_ada.shape[0]):
        x2 = _layer(x2, c_pad, batch, seq, norm_mix_pre[l], norm_mix_post[l], norm_ffn_pre[l],
                    norm_ffn_post[l], w_ada[l], b_ada[l], w_in[l], rel_bias[l], w_attn_out[l],
                    w_dw[l], b_dw[l], conv_ln_g[l], conv_ln_b[l], w_conv_out[l], w_out[l],
                    w_router[l], b_router[l], w_gate_up[l], b_gate_up[l], w_down[l], b_down[l])
    return x2.reshape(batch, seq, d)
```

```python
import functools

import numpy as np

import jax
import jax.numpy as jnp
from jax import lax
from jax.experimental import pallas as pl
from jax.experimental.pallas import tpu as pltpu

F32 = jnp.float32
BF16 = jnp.bfloat16
I32 = jnp.int32

CHUNK = 64
LEFT_CHUNKS = 8
HEAD_DIM = 64
TOP_K = 4
SWIGLU_LIMIT = 7.0
SWIGLU_ALPHA = 1.702
RMS_EPS = 1e-6
LN_EPS = 1e-5
MASK_VALUE = -1e30

LANES = 128
SUBLANES = 8
ROW_TILE = 16
VMEM_LIMIT_BYTES = 56 * 1024 * 1024

ADA_TN = 1024
INPROJ_TM = 1024
INPROJ_TN = 1024
ATTN_TQ = 256
CONV_TS = 128
CONV_CG = 256
CONV_HALO = 32
MIX_TM = 256
TOPK_TM = 1024
SLOT_TM = 512
MOE_TM = 256
GATE_UP_TN = 1024
ROW_TT = 256
ROW_UNROLL = 4
WEIGHT_DMA_PRIORITY = 1


def _params(sem, vmem=VMEM_LIMIT_BYTES):
    return pltpu.CompilerParams(dimension_semantics=sem, vmem_limit_bytes=vmem)


def _rms(x, eps):
    return x * lax.rsqrt(jnp.mean(x * x, axis=-1, keepdims=True) + eps)


def _store_row_tiles(ref, value, scratch):
    rows = value.shape[0]
    for s in range(ROW_TILE):
        scratch[s // SUBLANES][pl.ds(s % SUBLANES, rows, stride=SUBLANES), :] = (
            value[:, s * LANES:(s + 1) * LANES])
    halves = [sc[...].reshape(rows, SUBLANES, LANES) for sc in scratch]
    ref[...] = jnp.concatenate(halves, axis=1).astype(BF16)


def _tiles_to_rows(tiles, scratch):
    rows = tiles.shape[0]
    for h, sc in enumerate(scratch):
        sc[...] = tiles[:, h * SUBLANES:(h + 1) * SUBLANES, :].reshape(rows * SUBLANES, LANES)
    return jnp.concatenate(
        [sc[pl.ds(s, rows, stride=SUBLANES), :] for sc in scratch for s in range(SUBLANES)],
        axis=1)


def _tile_scratch(rows):
    return [pltpu.VMEM((rows * SUBLANES, LANES), F32)] * (ROW_TILE // SUBLANES)


def _ada_kernel(c_ref, w_ref, b_ref, o_ref):
    c = c_ref[...]
    s = c * jax.nn.sigmoid(c)
    o_ref[...] = jnp.dot(s.astype(BF16), w_ref[...].astype(BF16),
                         preferred_element_type=F32) + b_ref[...]


def _ada(c_pad, w_ada, b_ada):
    rows, d = c_pad.shape
    n_out = w_ada.shape[1]
    return pl.pallas_call(
        _ada_kernel,
        grid=(n_out // ADA_TN,),
        in_specs=[pl.BlockSpec((rows, d), lambda j: (0, 0)),
                  pl.BlockSpec((d, ADA_TN), lambda j: (0, j)),
                  pl.BlockSpec((1, ADA_TN), lambda j: (0, j))],
        out_specs=pl.BlockSpec((rows, ADA_TN), lambda j: (0, j)),
        out_shape=jax.ShapeDtypeStruct((rows, n_out), F32),
        compiler_params=_params(("arbitrary",)),
        name="ada",
    )(c_pad, w_ada, b_ada)


def _inproj_kernel(x_ref, g_ref, sc_ref, sh_ref, w_ref, o_ref, h_ref):
    @pl.when(pl.program_id(1) == 0)
    def _():
        y = _rms(x_ref[...], RMS_EPS) * g_ref[...]
        h_ref[...] = (y * (1.0 + sc_ref[...]) + sh_ref[...]).astype(BF16)

    o_ref[...] = jnp.dot(h_ref[...], w_ref[...], preferred_element_type=F32).astype(BF16)


def _inproj(x2, g, sc, sh, w_bf16, seq):
    n, d = x2.shape
    width = w_bf16.shape[1]
    tm, tn = INPROJ_TM, INPROJ_TN
    mod_spec = pl.BlockSpec((None, 1, d), lambda i, j: ((i * tm) // seq, 0, 0))
    return pl.pallas_call(
        _inproj_kernel,
        grid=(n // tm, width // tn),
        in_specs=[pl.BlockSpec((tm, d), lambda i, j: (i, 0)),
                  pl.BlockSpec((1, d), lambda i, j: (0, 0)),
                  mod_spec, mod_spec,
                  pl.BlockSpec((d, tn), lambda i, j: (0, j))],
        out_specs=pl.BlockSpec((tm, tn), lambda i, j: (i, j)),
        out_shape=jax.ShapeDtypeStruct((n, width), BF16),
        scratch_shapes=[pltpu.VMEM((tm, d), BF16)],
        compiler_params=_params(("arbitrary", "arbitrary")),
        name="inproj",
    )(x2, g, sc, sh, w_bf16)


def _attn_kernel(q_ref, k0_ref, k1_ref, k2_ref, v0_ref, v1_ref, v2_ref, g_ref, mask_ref,
                 o_ref, bias_ref):
    heads = g_ref.shape[0]
    tq = q_ref.shape[0]
    win = mask_ref.shape[-1]

    @pl.when(jnp.logical_and(pl.program_id(0) == 0, pl.program_id(1) == 0))
    def _():
        for h in range(heads):
            rows = jnp.broadcast_to(g_ref[h], (tq, g_ref.shape[-1]))
            bias_ref[h] = pltpu.roll(rows, 0, 1, stride=1, stride_axis=0)[:, :win]

    scale = HEAD_DIM ** -0.5
    per_vreg = LANES // HEAD_DIM
    for hp in range(heads // per_vreg):
        cols = slice(hp * LANES, (hp + 1) * LANES)
        q = q_ref[:, cols]
        k = jnp.concatenate([k0_ref[:, cols], k1_ref[:, cols], k2_ref[:, cols]], axis=0)
        v = jnp.concatenate([v0_ref[:, cols], v1_ref[:, cols], v2_ref[:, cols]], axis=0)
        outs = []
        for hh in range(per_vreg):
            sl = slice(hh * HEAD_DIM, (hh + 1) * HEAD_DIM)
            s = lax.dot_general(q[:, sl], k[:, sl], (((1,), (1,)), ((), ())),
                                preferred_element_type=F32)
            s = s * scale + bias_ref[hp * per_vreg + hh] + mask_ref[...]
            m = jnp.max(s, axis=-1, keepdims=True)
            p = jnp.exp(s - m)
            l = jnp.sum(p, axis=-1, keepdims=True)
            o = jnp.dot(p.astype(BF16), v[:, sl], preferred_element_type=F32)
            outs.append(o / l)
        o_ref[:, cols] = jnp.concatenate(outs, axis=1).astype(BF16)


def _attn_tables(rel_bias, tq, left, max_rel):
    win = left + tq
    length = pl.next_power_of_2(win + tq)
    u = np.arange(length)
    signed = np.where(u < win, u, u - length)
    idx = np.clip(left - signed, -max_rel, max_rel) + max_rel
    g = rel_bias[:, idx].astype(F32).reshape(rel_bias.shape[0], 1, length)

    r = np.arange(tq)[:, None]
    c = np.arange(win)[None, :]
    band = (c // CHUNK >= r // CHUNK) & (c // CHUNK <= r // CHUNK + LEFT_CHUNKS)
    masks = [np.where(band & (c >= left - var * tq), 0.0, MASK_VALUE)
             for var in range(left // tq + 1)]
    return g, jnp.asarray(np.stack(masks), F32)


def _attention(proj, g, masks, batch, seq, attn_width, q_off):
    n = proj.shape[0]
    tq = ATTN_TQ
    nq = seq // tq
    heads = g.shape[0]
    win = masks.shape[-1]
    nvar = masks.shape[0]
    nback = nvar - 1
    qcol = q_off // attn_width
    blk = (tq, attn_width)

    def kv_spec(col, back):
        return pl.BlockSpec(blk, lambda b, qi: (b * nq + jnp.maximum(qi - back, 0), col))

    in_specs = [pl.BlockSpec(blk, lambda b, qi: (b * nq + qi, qcol))]
    in_specs += [kv_spec(qcol + 1, back) for back in range(nback, -1, -1)]
    in_specs += [kv_spec(qcol + 2, back) for back in range(nback, -1, -1)]
    in_specs += [pl.BlockSpec(g.shape, lambda b, qi: (0, 0, 0)),
                 pl.BlockSpec((None, tq, win), lambda b, qi: (jnp.minimum(qi, nvar - 1), 0, 0))]
    return pl.pallas_call(
        _attn_kernel,
        grid=(batch, nq),
        in_specs=in_specs,
        out_specs=pl.BlockSpec(blk, lambda b, qi: (b * nq + qi, 0)),
        out_shape=jax.ShapeDtypeStruct((n, attn_width), BF16),
        scratch_shapes=[pltpu.VMEM((heads, tq, win), F32)],
        compiler_params=_params(("arbitrary", "arbitrary")),
        name="attn",
    )(proj, *([proj] * (2 * (nback + 1))), g, masks)


def _conv_kernel(ca_ref, cb_ref, w_ref, b_ref, g_ref, beta_ref, o_ref, ubuf, ush, cbuf, *, taps):
    ts = ca_ref.shape[0]
    width = ca_ref.shape[1]
    halo = CONV_HALO
    first = pl.program_id(1) == 0

    @pl.when(first)
    def _():
        ubuf[0:halo, :] = jnp.zeros((halo, width), F32)

    @pl.when(jnp.logical_not(first))
    def _():
        ubuf[0:halo, :] = ubuf[ts:ts + halo, :]

    ubuf[halo:halo + ts, :] = (ca_ref[...].astype(F32)
                               * jax.nn.sigmoid(cb_ref[...].astype(F32)))
    span = ush.shape[1]
    for s in range(1, SUBLANES):
        ush[s - 1] = ubuf[s:s + span, :]
    off = halo - (taps - 1)
    for cg in range(width // CONV_CG):
        cs = slice(cg * CONV_CG, (cg + 1) * CONV_CG)
        acc = jnp.zeros((ts, CONV_CG), F32)
        for j in range(taps):
            base, s = divmod(off + j, SUBLANES)
            base *= SUBLANES
            src = ubuf[base:base + ts, cs] if s == 0 else ush[s - 1, base:base + ts, cs]
            acc = acc + w_ref[j:j + 1, cs] * src
        cbuf[:, cs] = acc + b_ref[:, cs]
    y = cbuf[...]
    mu = jnp.mean(y, axis=-1, keepdims=True)
    yc = y - mu
    var = jnp.mean(yc * yc, axis=-1, keepdims=True)
    yn = yc * lax.rsqrt(var + LN_EPS) * g_ref[...] + beta_ref[...]
    o_ref[...] = (yn * jax.nn.sigmoid(yn)).astype(BF16)


def _conv(proj, w_dw_pad, b_dw, ln_g, ln_b, batch, seq, col_a, col_b, width, taps):
    n = proj.shape[0]
    ts = CONV_TS
    ns = seq // ts
    vec = pl.BlockSpec((1, width), lambda b, i: (0, 0))
    return pl.pallas_call(
        functools.partial(_conv_kernel, taps=taps),
        grid=(batch, ns),
        in_specs=[pl.BlockSpec((ts, width), lambda b, i: (b * ns + i, col_a)),
                  pl.BlockSpec((ts, width), lambda b, i: (b * ns + i, col_b)),
                  pl.BlockSpec(w_dw_pad.shape, lambda b, i: (0, 0)),
                  vec, vec, vec],
        out_specs=pl.BlockSpec((ts, width), lambda b, i: (b * ns + i, 0)),
        out_shape=jax.ShapeDtypeStruct((n, width), BF16),
        scratch_shapes=[pltpu.VMEM((CONV_HALO + ts, width), F32),
                        pltpu.VMEM((SUBLANES - 1, CONV_HALO + ts - SUBLANES, width), F32),
                        pltpu.VMEM((ts, width), F32)],
        compiler_params=_params(("arbitrary", "arbitrary")),
        name="conv",
    )(proj, proj, w_dw_pad, b_dw, ln_g, ln_b)


def _mix_kernel(attn_ref, cv_ref, ga0_ref, ga1_ref, gc0_ref, gc1_ref, x_ref, wa_ref, wc_ref,
                wo_ref, gpost_ref, g1_ref, gpre_ref, sc_ref, sh_ref, wr_ref, br_ref,
                x1_ref, hp_ref, lg_ref, *tile_scratch):
    ya = jnp.dot(attn_ref[...], wa_ref[...], preferred_element_type=F32)
    yc = jnp.dot(cv_ref[...], wc_ref[...], preferred_element_type=F32)
    ga = jnp.concatenate([ga0_ref[...], ga1_ref[...]], axis=1).astype(F32)
    gc = jnp.concatenate([gc0_ref[...], gc1_ref[...]], axis=1).astype(F32)
    merged = jax.nn.sigmoid(ga) * ya + jax.nn.sigmoid(gc) * yc
    y = jnp.dot(merged.astype(BF16), wo_ref[...], preferred_element_type=F32)
    x1 = x_ref[...] + g1_ref[...] * (_rms(y, RMS_EPS) * gpost_ref[...])
    x1_ref[...] = x1
    h = _rms(x1, RMS_EPS) * gpre_ref[...] * (1.0 + sc_ref[...]) + sh_ref[...]
    _store_row_tiles(hp_ref, h, tile_scratch)
    lg_ref[...] = jnp.dot(h.astype(BF16), wr_ref[...], preferred_element_type=F32) + br_ref[...]


def _mix(attn, cv, proj, x2, wa, wc, wo, gpost, g1, gpre, sc, sh, wr, br, seq, col_g):
    n, d = x2.shape
    tm = MIX_TM
    aw = attn.shape[1]
    cw = cv.shape[1]
    ne = wr.shape[1]
    gw = d // 2
    const = lambda shape: pl.BlockSpec(shape, lambda i: (0, 0), pipeline_mode=pl.Buffered(1))
    mod_spec = pl.BlockSpec((None, 1, d), lambda i: ((i * tm) // seq, 0, 0))
    gate_spec = lambda col: pl.BlockSpec((tm, gw), lambda i: (i, col))
    return pl.pallas_call(
        _mix_kernel,
        grid=(n // tm,),
        in_specs=[pl.BlockSpec((tm, aw), lambda i: (i, 0)),
                  pl.BlockSpec((tm, cw), lambda i: (i, 0)),
                  gate_spec(col_g), gate_spec(col_g + 1),
                  gate_spec(col_g + 2), gate_spec(col_g + 3),
                  pl.BlockSpec((tm, d), lambda i: (i, 0)),
                  const((aw, d)), const((cw, d)), const((d, d)),
                  const((1, d)), mod_spec, const((1, d)), mod_spec, mod_spec,
                  const((d, ne)), const((1, ne))],
        out_specs=[pl.BlockSpec((tm, d), lambda i: (i, 0)),
                   pl.BlockSpec((tm, ROW_TILE, LANES), lambda i: (i, 0, 0)),
                   pl.BlockSpec((tm, ne), lambda i: (i, 0))],
        out_shape=[jax.ShapeDtypeStruct((n, d), F32),
                   jax.ShapeDtypeStruct((n, ROW_TILE, LANES), BF16),
                   jax.ShapeDtypeStruct((n, ne), F32)],
        scratch_shapes=_tile_scratch(tm),
        compiler_params=_params(("arbitrary",)),
        name="mix",
    )(attn, cv, proj, proj, proj, proj, x2, wa, wc, wo, gpost, g1, gpre, sc, sh, wr, br)


def _columns_to_lanes(cols, dtype):
    tm = cols[0].shape[0]
    lane = lax.broadcasted_iota(I32, (tm, len(cols)), 1)
    out = jnp.zeros((tm, len(cols)), dtype)
    for k, col in enumerate(cols):
        out = jnp.where(lane == k, col, out)
    return out


def _topk_kernel(lg_ref, idx_ref, p_ref, cnt_ref):
    logits = lg_ref[...]
    tm, ne = logits.shape
    lane = lax.broadcasted_iota(I32, (tm, ne), 1)
    work = logits
    vals, idxs = [], []
    chosen = jnp.zeros((tm, ne), F32)
    for _ in range(TOP_K):
        m = jnp.max(work, axis=-1, keepdims=True)
        idx = jnp.min(jnp.where(work == m, lane, ne), axis=-1, keepdims=True)
        sel = lane == idx
        vals.append(m)
        idxs.append(idx)
        chosen = chosen + sel.astype(F32)
        work = jnp.where(sel, -jnp.inf, work)
    exps = [jnp.exp(v - vals[0]) for v in vals]
    denom = exps[0]
    for e in exps[1:]:
        denom = denom + e
    idx_ref[...] = _columns_to_lanes(idxs, I32)
    p_ref[...] = _columns_to_lanes([e / denom for e in exps], F32)

    @pl.when(pl.program_id(0) == 0)
    def _():
        cnt_ref[...] = jnp.zeros_like(cnt_ref)

    cnt_ref[...] += jnp.sum(chosen, axis=0, keepdims=True)


def _topk(logits):
    n, ne = logits.shape
    tm = TOPK_TM
    return pl.pallas_call(
        _topk_kernel,
        grid=(n // tm,),
        in_specs=[pl.BlockSpec((tm, ne), lambda i: (i, 0))],
        out_specs=[pl.BlockSpec((tm, TOP_K), lambda i: (i, 0)),
                   pl.BlockSpec((tm, TOP_K), lambda i: (i, 0)),
                   pl.BlockSpec((1, ne), lambda i: (0, 0))],
        out_shape=[jax.ShapeDtypeStruct((n, TOP_K), I32),
                   jax.ShapeDtypeStruct((n, TOP_K), F32),
                   jax.ShapeDtypeStruct((1, ne), F32)],
        compiler_params=_params(("arbitrary",)),
        name="topk",
    )(logits)


def _slots_kernel(idx_ref, start_ref, dest_ref, carry_ref):
    @pl.when(pl.program_id(0) == 0)
    def _():
        carry_ref[...] = jnp.zeros_like(carry_ref)

    idx = idx_ref[...]
    tm = idx.shape[0]
    ne = start_ref.shape[1]
    lane = lax.broadcasted_iota(I32, (tm, ne), 1)
    sels = [lane == idx[:, k:k + 1] for k in range(TOP_K)]
    chosen = jnp.zeros((tm, ne), F32)
    for sel in sels:
        chosen = chosen + sel.astype(F32)
    row = lax.broadcasted_iota(I32, (tm, tm), 0)
    col = lax.broadcasted_iota(I32, (tm, tm), 1)
    earlier = (col < row).astype(BF16)
    rank = jnp.dot(earlier, chosen.astype(BF16), preferred_element_type=F32)
    slot = start_ref[...] + carry_ref[...] + rank
    dests = [jnp.sum(jnp.where(sel, slot, 0.0), axis=-1, keepdims=True).astype(I32)
             for sel in sels]
    dest_ref[...] = _columns_to_lanes(dests, I32)
    carry_ref[...] += jnp.sum(chosen, axis=0, keepdims=True)


def _slots(idx, start_f32):
    n = idx.shape[0]
    ne = start_f32.shape[1]
    tm = SLOT_TM
    return pl.pallas_call(
        _slots_kernel,
        grid=(n // tm,),
        in_specs=[pl.BlockSpec((tm, TOP_K), lambda i: (i, 0)),
                  pl.BlockSpec((1, ne), lambda i: (0, 0))],
        out_specs=pl.BlockSpec((tm, TOP_K), lambda i: (i, 0)),
        out_shape=jax.ShapeDtypeStruct((n, TOP_K), I32),
        scratch_shapes=[pltpu.VMEM((1, ne), F32)],
        compiler_params=_params(("arbitrary",)),
        name="slots",
    )(idx, start_f32)


def _scatter_kernel(dest_ref, cnt_ref, start_ref, nu_ref, h_ref, xs_ref, zero_ref, sem, zsem,
                    *, ne, n_tiles):
    tt = h_ref.shape[0]

    def row_copy(r, k):
        return pltpu.make_async_copy(h_ref.at[r], xs_ref.at[dest_ref[r * TOP_K + k]], sem)

    def start_rows(r, carry):
        for k in range(TOP_K):
            row_copy(r, k).start(priority=k % 2)
        return carry

    def wait_rows(r, carry):
        for k in range(TOP_K):
            row_copy(r, k).wait()
        return carry

    lax.fori_loop(0, tt, start_rows, 0, unroll=ROW_UNROLL)

    @pl.when(pl.program_id(0) == 0)
    def _():
        zero_ref[...] = jnp.zeros_like(zero_ref)

        def pad_copy(s):
            return pltpu.make_async_copy(zero_ref.at[0], xs_ref.at[s], zsem)

        def tile_copy(t):
            dst = xs_ref.at[pl.ds(pl.multiple_of(t * MOE_TM, MOE_TM), MOE_TM)]
            return pltpu.make_async_copy(zero_ref, dst, zsem)

        def loop(lo, hi, fn):
            def body(i, c):
                fn(i)
                return c
            lax.fori_loop(lo, hi, body, 0)

        def per_expert(e, carry):
            cnt = cnt_ref[e]
            lo = start_ref[e] + cnt
            hi = start_ref[e] + ((cnt + MOE_TM - 1) // MOE_TM) * MOE_TM
            loop(lo, hi, lambda s: pad_copy(s).start())
            loop(lo, hi, lambda s: pad_copy(s).wait())
            return carry

        lax.fori_loop(0, ne, per_expert, 0)
        loop(nu_ref[0], n_tiles, lambda t: tile_copy(t).start())
        loop(nu_ref[0], n_tiles, lambda t: tile_copy(t).wait())

    lax.fori_loop(0, tt, wait_rows, 0, unroll=ROW_UNROLL)


def _scatter_rows(dest_flat, counts, starts, n_used, h_tiles, n_rows):
    ne = counts.shape[0]
    tt = ROW_TT
    n = h_tiles.shape[0]
    smem = pl.BlockSpec(memory_space=pltpu.SMEM)
    return pl.pallas_call(
        functools.partial(_scatter_kernel, ne=ne, n_tiles=n_rows // MOE_TM),
        grid=(n // tt,),
        in_specs=[pl.BlockSpec((tt * TOP_K,), lambda i: (i,), memory_space=pltpu.SMEM),
                  smem, smem, smem,
                  pl.BlockSpec((tt, ROW_TILE, LANES), lambda i: (i, 0, 0))],
        out_specs=pl.BlockSpec(memory_space=pl.ANY),
        out_shape=jax.ShapeDtypeStruct((n_rows, ROW_TILE, LANES), BF16),
        scratch_shapes=[pltpu.VMEM((MOE_TM, ROW_TILE, LANES), BF16),
                        pltpu.SemaphoreType.DMA, pltpu.SemaphoreType.DMA],
        compiler_params=_params(("arbitrary",)),
        name="scatter_rows",
    )(dest_flat, counts, starts, n_used, h_tiles)


def _new_expert(te_ref, t):
    prev = te_ref[jnp.maximum(t - 1, 0)]
    return jnp.logical_or(t == 0, te_ref[t] != prev)


def _stream_weights(t, pass_idx, n_pass, te_ref, nxt_ref, ord_ref, sc_ref, fetch):
    n_groups = sc_ref[1]
    ordinal = ord_ref[t]
    group = ordinal + pass_idx * n_groups
    slot = lax.rem(group, 2)

    @pl.when(_new_expert(te_ref, t))
    def _():
        @pl.when(group == 0)
        def _():
            for cp in fetch(te_ref[t], pass_idx, slot):
                cp.start(priority=WEIGHT_DMA_PRIORITY)

        for cp in fetch(te_ref[t], pass_idx, slot):
            cp.wait()
        last = ordinal == n_groups - 1

        @pl.when(jnp.logical_not(jnp.logical_and(last, pass_idx == n_pass - 1)))
        def _():
            for cp in fetch(nxt_ref[t], jnp.where(last, pass_idx + 1, pass_idx), 1 - slot):
                cp.start(priority=WEIGHT_DMA_PRIORITY)

    return slot


def _gate_up_kernel(te_ref, nxt_ref, ord_ref, sc_ref, xs_ref, w_ref, bg_ref, bu_ref, a_ref,
                    wg_buf, wu_buf, *scratch):
    tile_scratch, sem = scratch[:-1], scratch[-1]
    t = pl.program_id(1)
    used = t < sc_ref[0]
    tn = a_ref.shape[1]
    f = w_ref.shape[2] // 2

    def fetch(e, jj, slot):
        col = pl.multiple_of(jj * tn, tn)
        return [pltpu.make_async_copy(w_ref.at[e, :, pl.ds(col, tn)], wg_buf.at[slot],
                                      sem.at[slot]),
                pltpu.make_async_copy(w_ref.at[e, :, pl.ds(f + col, tn)], wu_buf.at[slot],
                                      sem.at[slot])]

    @pl.when(used)
    def _():
        slot = _stream_weights(t, pl.program_id(0), pl.num_programs(0), te_ref, nxt_ref,
                               ord_ref, sc_ref, fetch)
        x = _tiles_to_rows(xs_ref[...].astype(F32), tile_scratch)
        gate = jnp.dot(x, wg_buf[slot], preferred_element_type=F32) + bg_ref[...]
        up = jnp.dot(x, wu_buf[slot], preferred_element_type=F32) + bu_ref[...]
        gate = jnp.minimum(gate, SWIGLU_LIMIT)
        up = jnp.clip(up, -SWIGLU_LIMIT, SWIGLU_LIMIT)
        glu = gate * jax.nn.sigmoid(SWIGLU_ALPHA * gate)
        a_ref[...] = ((up + 1.0) * glu).astype(BF16)

    @pl.when(jnp.logical_not(used))
    def _():
        a_ref[...] = jnp.zeros_like(a_ref)


def _used_tile(t, sc):
    return jnp.minimum(t, sc[0] - 1)


def _gate_up(tiles, xs, w_gate_up, b_gate_up3):
    ne, d, two_f = w_gate_up.shape
    f = two_f // 2
    tm, tn = MOE_TM, GATE_UP_TN
    nt = xs.shape[0] // tm
    nj = f // tn
    expert = lambda t, te, sc: te[_used_tile(t, sc)]
    grid_spec = pltpu.PrefetchScalarGridSpec(
        num_scalar_prefetch=4,
        grid=(nj, nt),
        in_specs=[
            pl.BlockSpec((tm, ROW_TILE, LANES),
                         lambda j, t, te, nx, od, sc: (_used_tile(t, sc), 0, 0)),
            pl.BlockSpec(memory_space=pl.ANY),
            pl.BlockSpec((None, 1, tn), lambda j, t, te, nx, od, sc: (expert(t, te, sc), 0, j)),
            pl.BlockSpec((None, 1, tn),
                         lambda j, t, te, nx, od, sc: (expert(t, te, sc), 0, nj + j)),
        ],
        out_specs=pl.BlockSpec((tm, tn), lambda j, t, te, nx, od, sc: (t, j)),
        scratch_shapes=[pltpu.VMEM((2, d, tn), F32), pltpu.VMEM((2, d, tn), F32),
                        *_tile_scratch(tm), pltpu.SemaphoreType.DMA((2,))],
    )
    return pl.pallas_call(
        _gate_up_kernel,
        grid_spec=grid_spec,
        out_shape=jax.ShapeDtypeStruct((nt * tm, f), BF16),
        compiler_params=_params(("arbitrary", "arbitrary")),
        name="gate_up",
    )(*tiles, xs, w_gate_up, b_gate_up3, b_gate_up3)


def _down_kernel(te_ref, nxt_ref, ord_ref, sc_ref, a_ref, w_ref, bd_ref, y_ref, wd_buf,
                 *scratch):
    tile_scratch, sem = scratch[:-1], scratch[-1]
    t = pl.program_id(0)
    used = t < sc_ref[0]

    def fetch(e, jj, slot):
        return [pltpu.make_async_copy(w_ref.at[e], wd_buf.at[slot], sem.at[slot])]

    @pl.when(used)
    def _():
        slot = _stream_weights(t, 0, 1, te_ref, nxt_ref, ord_ref, sc_ref, fetch)
        y = jnp.dot(a_ref[...].astype(F32), wd_buf[slot],
                    preferred_element_type=F32) + bd_ref[...]
        _store_row_tiles(y_ref, y, tile_scratch)

    @pl.when(jnp.logical_not(used))
    def _():
        y_ref[...] = jnp.zeros_like(y_ref)


def _down(tiles, act, w_down, b_down3):
    n_rows, f = act.shape
    ne, _, d = w_down.shape
    tm = MOE_TM
    nt = n_rows // tm
    grid_spec = pltpu.PrefetchScalarGridSpec(
        num_scalar_prefetch=4,
        grid=(nt,),
        in_specs=[
            pl.BlockSpec((tm, f), lambda t, te, nx, od, sc: (_used_tile(t, sc), 0)),
            pl.BlockSpec(memory_space=pl.ANY),
            pl.BlockSpec((None, 1, d), lambda t, te, nx, od, sc: (te[_used_tile(t, sc)], 0, 0)),
        ],
        out_specs=pl.BlockSpec((tm, ROW_TILE, LANES), lambda t, te, nx, od, sc: (t, 0, 0)),
        scratch_shapes=[pltpu.VMEM((2, f, d), F32), *_tile_scratch(tm),
                        pltpu.SemaphoreType.DMA((2,))],
    )
    return pl.pallas_call(
        _down_kernel,
        grid_spec=grid_spec,
        out_shape=jax.ShapeDtypeStruct((n_rows, ROW_TILE, LANES), BF16),
        compiler_params=_params(("arbitrary",)),
        name="down",
    )(*tiles, act, w_down, b_down3)


def _combine_kernel(dest_ref, dest_next_ref, p_ref, x1_ref, gpost_ref, g2_ref, y_ref, o_ref,
                    *scratch):
    tt = x1_ref.shape[0]
    bufs = (scratch[:TOP_K], scratch[TOP_K:2 * TOP_K])
    tile_scratch, sem = scratch[2 * TOP_K:-1], scratch[-1]
    step = pl.program_id(0)

    def rows(dref, slot, fn):
        def body(r, carry):
            for k in range(TOP_K):
                fn(pltpu.make_async_copy(y_ref.at[dref[r * TOP_K + k]],
                                         bufs[slot][k].at[r], sem.at[slot]), k)
            return carry
        lax.fori_loop(0, tt, body, 0, unroll=ROW_UNROLL)

    for slot in range(2):
        @pl.when(lax.rem(step, 2) == slot)
        def _(slot=slot):
            if slot == 0:
                @pl.when(step == 0)
                def _():
                    rows(dest_ref, 0, lambda cp, k: cp.start(priority=k % 2))

            @pl.when(step + 1 < pl.num_programs(0))
            def _():
                rows(dest_next_ref, 1 - slot, lambda cp, k: cp.start(priority=k % 2))

            rows(dest_ref, slot, lambda cp, k: cp.wait())
            for k in range(TOP_K):
                rows_k = p_ref[k] * bufs[slot][k][...].astype(F32)
                tiles = rows_k if k == 0 else tiles + rows_k
            y = _tiles_to_rows(tiles, tile_scratch)
            o_ref[...] = x1_ref[...] + g2_ref[...] * (_rms(y, RMS_EPS) * gpost_ref[...])


def _combine(dest_flat, probs, x1, gpost, g2, y_tiles, seq):
    n, d = x1.shape
    tt = ROW_TT
    last = n // tt - 1
    p_lanes = jnp.broadcast_to(probs.T[:, :, None, None], (TOP_K, n, 1, LANES))
    return pl.pallas_call(
        _combine_kernel,
        grid=(n // tt,),
        in_specs=[pl.BlockSpec((tt * TOP_K,), lambda i: (i,), memory_space=pltpu.SMEM),
                  pl.BlockSpec((tt * TOP_K,), lambda i: (jnp.minimum(i + 1, last),),
                               memory_space=pltpu.SMEM),
                  pl.BlockSpec((TOP_K, tt, 1, LANES), lambda i: (0, i, 0, 0)),
                  pl.BlockSpec((tt, d), lambda i: (i, 0)),
                  pl.BlockSpec((1, d), lambda i: (0, 0)),
                  pl.BlockSpec((None, 1, d), lambda i: ((i * tt) // seq, 0, 0)),
                  pl.BlockSpec(memory_space=pl.ANY)],
        out_specs=pl.BlockSpec((tt, d), lambda i: (i, 0)),
        out_shape=jax.ShapeDtypeStruct((n, d), F32),
        scratch_shapes=[pltpu.VMEM((tt, ROW_TILE, LANES), BF16)] * (2 * TOP_K)
                       + _tile_scratch(tt) + [pltpu.SemaphoreType.DMA((2,))],
        compiler_params=_params(("arbitrary",)),
        name="combine",
    )(dest_flat, dest_flat, p_lanes, x1, gpost, g2, y_tiles)


def _layer(x2, c_pad, batch, seq, norm_mix_pre, norm_mix_post, norm_ffn_pre, norm_ffn_post,
           w_ada, b_ada, w_in, rel_bias, w_attn_out, w_dw, b_dw, conv_ln_g, conv_ln_b,
           w_conv_out, w_out, w_router, b_router, w_gate_up, b_gate_up, w_down, b_down):
    n, d = x2.shape
    attn_width = w_attn_out.shape[0]
    conv_width = w_conv_out.shape[0]
    taps = w_dw.shape[0]
    max_rel = (rel_bias.shape[1] - 1) // 2
    ne = w_router.shape[1]
    assert d == ROW_TILE * LANES, "row tiles assume 16 * 128 features per token"
    row = lambda v: v.reshape(1, -1)

    mod = _ada(c_pad, w_ada, row(b_ada))[:batch]
    sh1, sc1, g1, sh2, sc2, g2 = [m.reshape(batch, 1, d) for m in jnp.split(mod, 6, axis=-1)]

    proj = _inproj(x2, row(norm_mix_pre), sc1, sh1, w_in.astype(BF16), seq)

    g, masks = _attn_tables(rel_bias, ATTN_TQ, LEFT_CHUNKS * CHUNK, max_rel)
    attn = _attention(proj, g, masks, batch, seq, attn_width, 0)

    col_a = 3 * attn_width // conv_width
    w_dw_pad = jnp.pad(w_dw, ((0, CONV_HALO - taps), (0, 0)))
    cv = _conv(proj, w_dw_pad, row(b_dw), row(conv_ln_g), row(conv_ln_b),
               batch, seq, col_a, col_a + 1, conv_width, taps)

    col_g = (3 * attn_width + 2 * conv_width) // (d // 2)
    x1, h_tiles, logits = _mix(
        attn, cv, proj, x2, w_attn_out.astype(BF16), w_conv_out.astype(BF16),
        w_out.astype(BF16), row(norm_mix_post), g1, row(norm_ffn_pre), sc2, sh2,
        w_router.astype(BF16), row(b_router), seq, col_g)

    idx, probs, counts_f = _topk(logits)
    counts = counts_f.reshape(ne).astype(I32)
    padded = ((counts + MOE_TM - 1) // MOE_TM) * MOE_TM
    ends = jnp.cumsum(padded)
    starts = ends - padded
    n_rows = n * TOP_K + ne * MOE_TM
    n_tiles = n_rows // MOE_TM
    tile_start = jnp.arange(n_tiles, dtype=I32) * MOE_TM
    tile_e = jnp.minimum(jnp.sum((tile_start[:, None] >= ends[None, :]).astype(I32), axis=1),
                         ne - 1)
    n_used = (ends[-1:] // MOE_TM).astype(I32)
    experts = jnp.arange(ne, dtype=I32)
    nonempty = counts > 0
    later = nonempty[None, :] & (experts[None, :] > experts[:, None])
    next_e = jnp.min(jnp.where(later, experts[None, :], ne), axis=1)
    next_e = jnp.where(next_e < ne, next_e, jnp.min(jnp.where(nonempty, experts, ne)))
    ordinal_e = jnp.cumsum(nonempty.astype(I32)) - 1
    n_groups = jnp.sum(nonempty.astype(I32), keepdims=True)
    tiles = (tile_e, next_e[tile_e].astype(I32), ordinal_e[tile_e].astype(I32),
             jnp.concatenate([n_used, n_groups]))

    dest = _slots(idx, starts.astype(F32).reshape(1, ne))
    dest_flat = dest.reshape(n * TOP_K)
    xs = _scatter_rows(dest_flat, counts, starts, n_used, h_tiles, n_rows)
    act = _gate_up(tiles, xs, w_gate_up, b_gate_up.reshape(ne, 1, -1))
    y_tiles = _down(tiles, act, w_down, b_down.reshape(ne, 1, -1))
    return _combine(dest_flat, probs, x1, row(norm_ffn_post), g2, y_tiles, seq)


def kernel(x, c, norm_mix_pre, norm_mix_post, norm_ffn_pre, norm_ffn_post, w_ada, b_ada, w_in,
           rel_bias, w_attn_out, w_dw, b_dw, conv_ln_g, conv_ln_b, w_conv_out, w_out, w_router,
           b_router, w_gate_up, b_gate_up, w_down, b_down):
    batch, seq, d = x.shape
    x2 = x.reshape(batch * seq, d)
    c_pad = jnp.pad(c, ((0, SUBLANES - batch % SUBLANES), (0, 0))) if batch % SUBLANES else c
    for l in range(w_ada.shape[0]):
        x2 = _layer(x2, c_pad, batch, seq, norm_mix_pre[l], norm_mix_post[l], norm_ffn_pre[l],
                    norm_ffn_post[l], w_ada[l], b_ada[l], w_in[l], rel_bias[l], w_attn_out[l],
                    w_dw[l], b_dw[l], conv_ln_g[l], conv_ln_b[l], w_conv_out[l], w_out[l],
                    w_router[l], b_router[l], w_gate_up[l], b_gate_up[l], w_down[l], b_down[l])
    return x2.reshape(batch, seq, d)
```

```python
import functools

import numpy as np

import jax
import jax.numpy as jnp
from jax import lax
from jax.experimental import pallas as pl
from jax.experimental.pallas import tpu as pltpu

F32 = jnp.float32
BF16 = jnp.bfloat16
I32 = jnp.int32

CHUNK = 64
LEFT_CHUNKS = 8
HEAD_DIM = 64
TOP_K = 4
SWIGLU_LIMIT = 7.0
SWIGLU_ALPHA = 1.702
RMS_EPS = 1e-6
LN_EPS = 1e-5
MASK_VALUE = -1e30

LANES = 128
SUBLANES = 8
ROW_TILE = 16
VMEM_LIMIT_BYTES = 56 * 1024 * 1024

ADA_TN = 1024
INPROJ_TM = 1024
INPROJ_TN = 1024
INPROJ_CHUNKS = 8
ATTN_TQ = 256
CONV_TS = 128
CONV_CG = 256
CONV_HALO = 32
MIX_TM = 256
TOPK_TM = 1024
SLOT_TM = 512
MOE_TM = 256
GATE_UP_TN = 1024
ROW_TT = 256
ROW_UNROLL = 4
WEIGHT_DMA_PRIORITY = 1


def _params(sem, vmem=VMEM_LIMIT_BYTES):
    return pltpu.CompilerParams(dimension_semantics=sem, vmem_limit_bytes=vmem)


def _rms(x, eps):
    return x * lax.rsqrt(jnp.mean(x * x, axis=-1, keepdims=True) + eps)


def _store_row_tiles(ref, value, scratch):
    rows = value.shape[0]
    for s in range(ROW_TILE):
        scratch[s // SUBLANES][pl.ds(s % SUBLANES, rows, stride=SUBLANES), :] = (
            value[:, s * LANES:(s + 1) * LANES])
    halves = [sc[...].reshape(rows, SUBLANES, LANES) for sc in scratch]
    ref[...] = jnp.concatenate(halves, axis=1).astype(BF16)


def _tiles_to_rows(tiles, scratch):
    rows = tiles.shape[0]
    for h, sc in enumerate(scratch):
        sc[...] = tiles[:, h * SUBLANES:(h + 1) * SUBLANES, :].reshape(rows * SUBLANES, LANES)
    return _rows_from_scratch(scratch, rows)


def _rows_from_scratch(scratch, rows):
    return jnp.concatenate(
        [sc[pl.ds(s, rows, stride=SUBLANES), :] for sc in scratch for s in range(SUBLANES)],
        axis=1)


def _tile_scratch(rows):
    return [pltpu.VMEM((rows * SUBLANES, LANES), F32)] * (ROW_TILE // SUBLANES)


def _ada_kernel(c_ref, w_ref, b_ref, o_ref):
    c = c_ref[...]
    s = c * jax.nn.sigmoid(c)
    o_ref[...] = jnp.dot(s.astype(BF16), w_ref[...].astype(BF16),
                         preferred_element_type=F32) + b_ref[...]


def _ada(c_pad, w_ada, b_ada):
    rows, d = c_pad.shape
    n_out = w_ada.shape[1]
    return pl.pallas_call(
        _ada_kernel,
        grid=(n_out // ADA_TN,),
        in_specs=[pl.BlockSpec((rows, d), lambda j: (0, 0)),
                  pl.BlockSpec((d, ADA_TN), lambda j: (0, j)),
                  pl.BlockSpec((1, ADA_TN), lambda j: (0, j))],
        out_specs=pl.BlockSpec((rows, ADA_TN), lambda j: (0, j)),
        out_shape=jax.ShapeDtypeStruct((rows, n_out), F32),
        compiler_params=_params(("arbitrary",)),
        name="ada",
    )(c_pad, w_ada, b_ada)


def _inproj_kernel(x0_ref, xn_ref, g_ref, sc0_ref, sh0_ref, scn_ref, shn_ref, w_ref, o_ref,
                   h_ref):
    i = pl.program_id(0)
    j = pl.program_id(1)
    tm = o_ref.shape[0]
    chunk = tm // INPROJ_CHUNKS

    def modulated(x, sc, sh):
        return ((_rms(x, RMS_EPS) * g_ref[...]) * (1.0 + sc) + sh).astype(BF16)

    @pl.when(jnp.logical_and(i == 0, j == 0))
    def _():
        h_ref[0] = modulated(x0_ref[...], sc0_ref[...], sh0_ref[...])

    c = jnp.minimum(j, INPROJ_CHUNKS - 1)
    rows = pl.ds(pl.multiple_of(c * chunk, chunk), chunk)
    h_ref[lax.rem(i + 1, 2), rows, :] = modulated(xn_ref[rows, :], scn_ref[...], shn_ref[...])
    o_ref[...] = jnp.dot(h_ref[lax.rem(i, 2)], w_ref[...],
                         preferred_element_type=F32).astype(BF16)


def _inproj(x2, g, sc, sh, w_bf16, seq):
    n, d = x2.shape
    width = w_bf16.shape[1]
    tm, tn = INPROJ_TM, INPROJ_TN
    assert width // tn >= INPROJ_CHUNKS
    nxt = lambda i: jnp.minimum(i + 1, n // tm - 1)
    first_mod = pl.BlockSpec((None, 1, d), lambda i, j: (0, 0, 0))
    next_mod = pl.BlockSpec((None, 1, d), lambda i, j: ((nxt(i) * tm) // seq, 0, 0))
    return pl.pallas_call(
        _inproj_kernel,
        grid=(n // tm, width // tn),
        in_specs=[pl.BlockSpec((tm, d), lambda i, j: (0, 0), pipeline_mode=pl.Buffered(1)),
                  pl.BlockSpec((tm, d), lambda i, j: (nxt(i), 0)),
                  pl.BlockSpec((1, d), lambda i, j: (0, 0)),
                  first_mod, first_mod, next_mod, next_mod,
                  pl.BlockSpec((d, tn), lambda i, j: (0, j))],
        out_specs=pl.BlockSpec((tm, tn), lambda i, j: (i, j)),
        out_shape=jax.ShapeDtypeStruct((n, width), BF16),
        scratch_shapes=[pltpu.VMEM((2, tm, d), BF16)],
        compiler_params=_params(("arbitrary", "arbitrary")),
        name="inproj",
    )(x2, x2, g, sc, sh, sc, sh, w_bf16)


def _attn_kernel(q_ref, k0_ref, k1_ref, k2_ref, v0_ref, v1_ref, v2_ref, g_ref, mask_ref,
                 o_ref, bias_ref):
    heads = g_ref.shape[0]
    tq = q_ref.shape[0]
    win = mask_ref.shape[-1]

    @pl.when(jnp.logical_and(pl.program_id(0) == 0, pl.program_id(1) == 0))
    def _():
        for h in range(heads):
            rows = jnp.broadcast_to(g_ref[h], (tq, g_ref.shape[-1]))
            bias_ref[h] = pltpu.roll(rows, 0, 1, stride=1, stride_axis=0)[:, :win]

    scale = HEAD_DIM ** -0.5
    per_vreg = LANES // HEAD_DIM
    for hp in range(heads // per_vreg):
        cols = slice(hp * LANES, (hp + 1) * LANES)
        q = q_ref[:, cols]
        k = jnp.concatenate([k0_ref[:, cols], k1_ref[:, cols], k2_ref[:, cols]], axis=0)
        v = jnp.concatenate([v0_ref[:, cols], v1_ref[:, cols], v2_ref[:, cols]], axis=0)
        outs = []
        for hh in range(per_vreg):
            sl = slice(hh * HEAD_DIM, (hh + 1) * HEAD_DIM)
            s = lax.dot_general(q[:, sl], k[:, sl], (((1,), (1,)), ((), ())),
                                preferred_element_type=F32)
            s = s * scale + bias_ref[hp * per_vreg + hh] + mask_ref[...]
            m = jnp.max(s, axis=-1, keepdims=True)
            p = jnp.exp(s - m)
            l = jnp.sum(p, axis=-1, keepdims=True)
            o = jnp.dot(p.astype(BF16), v[:, sl], preferred_element_type=F32)
            outs.append(o / l)
        o_ref[:, cols] = jnp.concatenate(outs, axis=1).astype(BF16)


def _attn_tables(rel_bias, tq, left, max_rel):
    win = left + tq
    length = pl.next_power_of_2(win + tq)
    u = np.arange(length)
    signed = np.where(u < win, u, u - length)
    idx = np.clip(left - signed, -max_rel, max_rel) + max_rel
    g = rel_bias[:, idx].astype(F32).reshape(rel_bias.shape[0], 1, length)

    r = np.arange(tq)[:, None]
    c = np.arange(win)[None, :]
    band = (c // CHUNK >= r // CHUNK) & (c // CHUNK <= r // CHUNK + LEFT_CHUNKS)
    masks = [np.where(band & (c >= left - var * tq), 0.0, MASK_VALUE)
             for var in range(left // tq + 1)]
    return g, jnp.asarray(np.stack(masks), F32)


def _attention(proj, g, masks, batch, seq, attn_width, q_off):
    n = proj.shape[0]
    tq = ATTN_TQ
    nq = seq // tq
    heads = g.shape[0]
    win = masks.shape[-1]
    nvar = masks.shape[0]
    nback = nvar - 1
    qcol = q_off // attn_width
    blk = (tq, attn_width)

    def kv_spec(col, back):
        return pl.BlockSpec(blk, lambda b, qi: (b * nq + jnp.maximum(qi - back, 0), col))

    in_specs = [pl.BlockSpec(blk, lambda b, qi: (b * nq + qi, qcol))]
    in_specs += [kv_spec(qcol + 1, back) for back in range(nback, -1, -1)]
    in_specs += [kv_spec(qcol + 2, back) for back in range(nback, -1, -1)]
    in_specs += [pl.BlockSpec(g.shape, lambda b, qi: (0, 0, 0)),
                 pl.BlockSpec((None, tq, win), lambda b, qi: (jnp.minimum(qi, nvar - 1), 0, 0))]
    return pl.pallas_call(
        _attn_kernel,
        grid=(batch, nq),
        in_specs=in_specs,
        out_specs=pl.BlockSpec(blk, lambda b, qi: (b * nq + qi, 0)),
        out_shape=jax.ShapeDtypeStruct((n, attn_width), BF16),
        scratch_shapes=[pltpu.VMEM((heads, tq, win), F32)],
        compiler_params=_params(("arbitrary", "arbitrary")),
        name="attn",
    )(proj, *([proj] * (2 * (nback + 1))), g, masks)


def _conv_kernel(ca_ref, cb_ref, w_ref, b_ref, g_ref, beta_ref, o_ref, ubuf, ush, cbuf, *, taps):
    ts = ca_ref.shape[0]
    width = ca_ref.shape[1]
    halo = CONV_HALO
    first = pl.program_id(1) == 0

    @pl.when(first)
    def _():
        ubuf[0:halo, :] = jnp.zeros((halo, width), F32)

    @pl.when(jnp.logical_not(first))
    def _():
        ubuf[0:halo, :] = ubuf[ts:ts + halo, :]

    ubuf[halo:halo + ts, :] = (ca_ref[...].astype(F32)
                               * jax.nn.sigmoid(cb_ref[...].astype(F32)))
    span = ush.shape[1]
    for s in range(1, SUBLANES):
        ush[s - 1] = ubuf[s:s + span, :]
    off = halo - (taps - 1)
    for cg in range(width // CONV_CG):
        cs = slice(cg * CONV_CG, (cg + 1) * CONV_CG)
        acc = jnp.zeros((ts, CONV_CG), F32)
        for j in range(taps):
            base, s = divmod(off + j, SUBLANES)
            base *= SUBLANES
            src = ubuf[base:base + ts, cs] if s == 0 else ush[s - 1, base:base + ts, cs]
            acc = acc + w_ref[j:j + 1, cs] * src
        cbuf[:, cs] = acc + b_ref[:, cs]
    y = cbuf[...]
    mu = jnp.mean(y, axis=-1, keepdims=True)
    yc = y - mu
    var = jnp.mean(yc * yc, axis=-1, keepdims=True)
    yn = yc * lax.rsqrt(var + LN_EPS) * g_ref[...] + beta_ref[...]
    o_ref[...] = (yn * jax.nn.sigmoid(yn)).astype(BF16)


def _conv(proj, w_dw_pad, b_dw, ln_g, ln_b, batch, seq, col_a, col_b, width, taps):
    n = proj.shape[0]
    ts = CONV_TS
    ns = seq // ts
    vec = pl.BlockSpec((1, width), lambda b, i: (0, 0))
    return pl.pallas_call(
        functools.partial(_conv_kernel, taps=taps),
        grid=(batch, ns),
        in_specs=[pl.BlockSpec((ts, width), lambda b, i: (b * ns + i, col_a)),
                  pl.BlockSpec((ts, width), lambda b, i: (b * ns + i, col_b)),
                  pl.BlockSpec(w_dw_pad.shape, lambda b, i: (0, 0)),
                  vec, vec, vec],
        out_specs=pl.BlockSpec((ts, width), lambda b, i: (b * ns + i, 0)),
        out_shape=jax.ShapeDtypeStruct((n, width), BF16),
        scratch_shapes=[pltpu.VMEM((CONV_HALO + ts, width), F32),
                        pltpu.VMEM((SUBLANES - 1, CONV_HALO + ts - SUBLANES, width), F32),
                        pltpu.VMEM((ts, width), F32)],
        compiler_params=_params(("arbitrary", "arbitrary")),
        name="conv",
    )(proj, proj, w_dw_pad, b_dw, ln_g, ln_b)


def _mix_kernel(attn_ref, cv_ref, ga0_ref, ga1_ref, gc0_ref, gc1_ref, x_ref, wa_ref, wc_ref,
                wo_ref, gpost_ref, g1_ref, gpre_ref, sc_ref, sh_ref, wr_ref, br_ref,
                x1_ref, hp_ref, lg_ref, *tile_scratch):
    ya = jnp.dot(attn_ref[...], wa_ref[...], preferred_element_type=F32)
    yc = jnp.dot(cv_ref[...], wc_ref[...], preferred_element_type=F32)
    ga = jnp.concatenate([ga0_ref[...], ga1_ref[...]], axis=1).astype(F32)
    gc = jnp.concatenate([gc0_ref[...], gc1_ref[...]], axis=1).astype(F32)
    merged = jax.nn.sigmoid(ga) * ya + jax.nn.sigmoid(gc) * yc
    y = jnp.dot(merged.astype(BF16), wo_ref[...], preferred_element_type=F32)
    x1 = x_ref[...] + g1_ref[...] * (_rms(y, RMS_EPS) * gpost_ref[...])
    x1_ref[...] = x1
    h = _rms(x1, RMS_EPS) * gpre_ref[...] * (1.0 + sc_ref[...]) + sh_ref[...]
    _store_row_tiles(hp_ref, h, tile_scratch)
    lg_ref[...] = jnp.dot(h.astype(BF16), wr_ref[...], preferred_element_type=F32) + br_ref[...]


def _mix(attn, cv, proj, x2, wa, wc, wo, gpost, g1, gpre, sc, sh, wr, br, seq, col_g):
    n, d = x2.shape
    tm = MIX_TM
    aw = attn.shape[1]
    cw = cv.shape[1]
    ne = wr.shape[1]
    gw = d // 2
    const = lambda shape: pl.BlockSpec(shape, lambda i: (0, 0), pipeline_mode=pl.Buffered(1))
    mod_spec = pl.BlockSpec((None, 1, d), lambda i: ((i * tm) // seq, 0, 0))
    gate_spec = lambda col: pl.BlockSpec((tm, gw), lambda i: (i, col))
    return pl.pallas_call(
        _mix_kernel,
        grid=(n // tm,),
        in_specs=[pl.BlockSpec((tm, aw), lambda i: (i, 0)),
                  pl.BlockSpec((tm, cw), lambda i: (i, 0)),
                  gate_spec(col_g), gate_spec(col_g + 1),
                  gate_spec(col_g + 2), gate_spec(col_g + 3),
                  pl.BlockSpec((tm, d), lambda i: (i, 0)),
                  const((aw, d)), const((cw, d)), const((d, d)),
                  const((1, d)), mod_spec, const((1, d)), mod_spec, mod_spec,
                  const((d, ne)), const((1, ne))],
        out_specs=[pl.BlockSpec((tm, d), lambda i: (i, 0)),
                   pl.BlockSpec((tm, ROW_TILE, LANES), lambda i: (i, 0, 0)),
                   pl.BlockSpec((tm, ne), lambda i: (i, 0))],
        out_shape=[jax.ShapeDtypeStruct((n, d), F32),
                   jax.ShapeDtypeStruct((n, ROW_TILE, LANES), BF16),
                   jax.ShapeDtypeStruct((n, ne), F32)],
        scratch_shapes=_tile_scratch(tm),
        compiler_params=_params(("arbitrary",)),
        name="mix",
    )(attn, cv, proj, proj, proj, proj, x2, wa, wc, wo, gpost, g1, gpre, sc, sh, wr, br)


def _columns_to_lanes(cols, dtype):
    tm = cols[0].shape[0]
    lane = lax.broadcasted_iota(I32, (tm, len(cols)), 1)
    out = jnp.zeros((tm, len(cols)), dtype)
    for k, col in enumerate(cols):
        out = jnp.where(lane == k, col, out)
    return out


def _topk_kernel(lg_ref, idx_ref, p_ref, cnt_ref):
    logits = lg_ref[...]
    tm, ne = logits.shape
    lane = lax.broadcasted_iota(I32, (tm, ne), 1)
    work = logits
    vals, idxs = [], []
    chosen = jnp.zeros((tm, ne), F32)
    for _ in range(TOP_K):
        m = jnp.max(work, axis=-1, keepdims=True)
        idx = jnp.min(jnp.where(work == m, lane, ne), axis=-1, keepdims=True)
        sel = lane == idx
        vals.append(m)
        idxs.append(idx)
        chosen = chosen + sel.astype(F32)
        work = jnp.where(sel, -jnp.inf, work)
    exps = [jnp.exp(v - vals[0]) for v in vals]
    denom = exps[0]
    for e in exps[1:]:
        denom = denom + e
    idx_ref[...] = _columns_to_lanes(idxs, I32)
    p_ref[...] = _columns_to_lanes([e / denom for e in exps], F32)

    @pl.when(pl.program_id(0) == 0)
    def _():
        cnt_ref[...] = jnp.zeros_like(cnt_ref)

    cnt_ref[...] += jnp.sum(chosen, axis=0, keepdims=True)


def _topk(logits):
    n, ne = logits.shape
    tm = TOPK_TM
    return pl.pallas_call(
        _topk_kernel,
        grid=(n // tm,),
        in_specs=[pl.BlockSpec((tm, ne), lambda i: (i, 0))],
        out_specs=[pl.BlockSpec((tm, TOP_K), lambda i: (i, 0)),
                   pl.BlockSpec((tm, TOP_K), lambda i: (i, 0)),
                   pl.BlockSpec((1, ne), lambda i: (0, 0))],
        out_shape=[jax.ShapeDtypeStruct((n, TOP_K), I32),
                   jax.ShapeDtypeStruct((n, TOP_K), F32),
                   jax.ShapeDtypeStruct((1, ne), F32)],
        compiler_params=_params(("arbitrary",)),
        name="topk",
    )(logits)


def _slots_kernel(idx_ref, start_ref, dest_ref, carry_ref):
    @pl.when(pl.program_id(0) == 0)
    def _():
        carry_ref[...] = jnp.zeros_like(carry_ref)

    idx = idx_ref[...]
    tm = idx.shape[0]
    ne = start_ref.shape[1]
    lane = lax.broadcasted_iota(I32, (tm, ne), 1)
    sels = [lane == idx[:, k:k + 1] for k in range(TOP_K)]
    chosen = jnp.zeros((tm, ne), F32)
    for sel in sels:
        chosen = chosen + sel.astype(F32)
    row = lax.broadcasted_iota(I32, (tm, tm), 0)
    col = lax.broadcasted_iota(I32, (tm, tm), 1)
    earlier = (col < row).astype(BF16)
    rank = jnp.dot(earlier, chosen.astype(BF16), preferred_element_type=F32)
    slot = start_ref[...] + carry_ref[...] + rank
    dests = [jnp.sum(jnp.where(sel, slot, 0.0), axis=-1, keepdims=True).astype(I32)
             for sel in sels]
    dest_ref[...] = _columns_to_lanes(dests, I32)
    carry_ref[...] += jnp.sum(chosen, axis=0, keepdims=True)


def _slots(idx, start_f32):
    n = idx.shape[0]
    ne = start_f32.shape[1]
    tm = SLOT_TM
    return pl.pallas_call(
        _slots_kernel,
        grid=(n // tm,),
        in_specs=[pl.BlockSpec((tm, TOP_K), lambda i: (i, 0)),
                  pl.BlockSpec((1, ne), lambda i: (0, 0))],
        out_specs=pl.BlockSpec((tm, TOP_K), lambda i: (i, 0)),
        out_shape=jax.ShapeDtypeStruct((n, TOP_K), I32),
        scratch_shapes=[pltpu.VMEM((1, ne), F32)],
        compiler_params=_params(("arbitrary",)),
        name="slots",
    )(idx, start_f32)


def _scatter_kernel(dest_ref, cnt_ref, start_ref, nu_ref, h_ref, xs_ref, zero_ref, sem, zsem,
                    *, ne, n_tiles):
    tt = h_ref.shape[0]

    def row_copy(r, k):
        return pltpu.make_async_copy(h_ref.at[r], xs_ref.at[dest_ref[r * TOP_K + k]], sem)

    def start_rows(r, carry):
        for k in range(TOP_K):
            row_copy(r, k).start(priority=k % 2)
        return carry

    def wait_rows(r, carry):
        for k in range(TOP_K):
            row_copy(r, k).wait()
        return carry

    lax.fori_loop(0, tt, start_rows, 0, unroll=ROW_UNROLL)

    @pl.when(pl.program_id(0) == 0)
    def _():
        zero_ref[...] = jnp.zeros_like(zero_ref)

        def pad_copy(s):
            return pltpu.make_async_copy(zero_ref.at[0], xs_ref.at[s], zsem)

        def tile_copy(t):
            dst = xs_ref.at[pl.ds(pl.multiple_of(t * MOE_TM, MOE_TM), MOE_TM)]
            return pltpu.make_async_copy(zero_ref, dst, zsem)

        def loop(lo, hi, fn):
            def body(i, c):
                fn(i)
                return c
            lax.fori_loop(lo, hi, body, 0)

        def per_expert(e, carry):
            cnt = cnt_ref[e]
            lo = start_ref[e] + cnt
            hi = start_ref[e] + ((cnt + MOE_TM - 1) // MOE_TM) * MOE_TM
            loop(lo, hi, lambda s: pad_copy(s).start())
            loop(lo, hi, lambda s: pad_copy(s).wait())
            return carry

        lax.fori_loop(0, ne, per_expert, 0)
        loop(nu_ref[0], n_tiles, lambda t: tile_copy(t).start())
        loop(nu_ref[0], n_tiles, lambda t: tile_copy(t).wait())

    lax.fori_loop(0, tt, wait_rows, 0, unroll=ROW_UNROLL)


def _scatter_rows(dest_flat, counts, starts, n_used, h_tiles, n_rows):
    ne = counts.shape[0]
    tt = ROW_TT
    n = h_tiles.shape[0]
    smem = pl.BlockSpec(memory_space=pltpu.SMEM)
    return pl.pallas_call(
        functools.partial(_scatter_kernel, ne=ne, n_tiles=n_rows // MOE_TM),
        grid=(n // tt,),
        in_specs=[pl.BlockSpec((tt * TOP_K,), lambda i: (i,), memory_space=pltpu.SMEM),
                  smem, smem, smem,
                  pl.BlockSpec((tt, ROW_TILE, LANES), lambda i: (i, 0, 0))],
        out_specs=pl.BlockSpec(memory_space=pl.ANY),
        out_shape=jax.ShapeDtypeStruct((n_rows, ROW_TILE, LANES), BF16),
        scratch_shapes=[pltpu.VMEM((MOE_TM, ROW_TILE, LANES), BF16),
                        pltpu.SemaphoreType.DMA, pltpu.SemaphoreType.DMA],
        compiler_params=_params(("arbitrary",)),
        name="scatter_rows",
    )(dest_flat, counts, starts, n_used, h_tiles)


def _new_expert(te_ref, t):
    prev = te_ref[jnp.maximum(t - 1, 0)]
    return jnp.logical_or(t == 0, te_ref[t] != prev)


def _stream_weights(t, pass_idx, n_pass, te_ref, nxt_ref, ord_ref, sc_ref, fetch):
    n_groups = sc_ref[1]
    ordinal = ord_ref[t]
    group = ordinal + pass_idx * n_groups
    slot = lax.rem(group, 2)

    @pl.when(_new_expert(te_ref, t))
    def _():
        @pl.when(group == 0)
        def _():
            for cp in fetch(te_ref[t], pass_idx, slot):
                cp.start(priority=WEIGHT_DMA_PRIORITY)

        for cp in fetch(te_ref[t], pass_idx, slot):
            cp.wait()
        last = ordinal == n_groups - 1

        @pl.when(jnp.logical_not(jnp.logical_and(last, pass_idx == n_pass - 1)))
        def _():
            for cp in fetch(nxt_ref[t], jnp.where(last, pass_idx + 1, pass_idx), 1 - slot):
                cp.start(priority=WEIGHT_DMA_PRIORITY)

    return slot


def _gate_up_kernel(te_ref, nxt_ref, ord_ref, sc_ref, xs_ref, w_ref, bg_ref, bu_ref, a_ref,
                    wg_buf, wu_buf, *scratch):
    tile_scratch, sem = scratch[:-1], scratch[-1]
    t = pl.program_id(1)
    used = t < sc_ref[0]
    tn = a_ref.shape[1]
    f = w_ref.shape[2] // 2

    def fetch(e, jj, slot):
        col = pl.multiple_of(jj * tn, tn)
        return [pltpu.make_async_copy(w_ref.at[e, :, pl.ds(col, tn)], wg_buf.at[slot],
                                      sem.at[slot]),
                pltpu.make_async_copy(w_ref.at[e, :, pl.ds(f + col, tn)], wu_buf.at[slot],
                                      sem.at[slot])]

    @pl.when(used)
    def _():
        slot = _stream_weights(t, pl.program_id(0), pl.num_programs(0), te_ref, nxt_ref,
                               ord_ref, sc_ref, fetch)
        x = _tiles_to_rows(xs_ref[...].astype(F32), tile_scratch)
        gate = jnp.dot(x, wg_buf[slot], preferred_element_type=F32) + bg_ref[...]
        up = jnp.dot(x, wu_buf[slot], preferred_element_type=F32) + bu_ref[...]
        gate = jnp.minimum(gate, SWIGLU_LIMIT)
        up = jnp.clip(up, -SWIGLU_LIMIT, SWIGLU_LIMIT)
        glu = gate * jax.nn.sigmoid(SWIGLU_ALPHA * gate)
        a_ref[...] = ((up + 1.0) * glu).astype(BF16)

    @pl.when(jnp.logical_not(used))
    def _():
        a_ref[...] = jnp.zeros_like(a_ref)


def _used_tile(t, sc):
    return jnp.minimum(t, sc[0] - 1)


def _gate_up(tiles, xs, w_gate_up, b_gate_up3):
    ne, d, two_f = w_gate_up.shape
    f = two_f // 2
    tm, tn = MOE_TM, GATE_UP_TN
    nt = xs.shape[0] // tm
    nj = f // tn
    expert = lambda t, te, sc: te[_used_tile(t, sc)]
    grid_spec = pltpu.PrefetchScalarGridSpec(
        num_scalar_prefetch=4,
        grid=(nj, nt),
        in_specs=[
            pl.BlockSpec((tm, ROW_TILE, LANES),
                         lambda j, t, te, nx, od, sc: (_used_tile(t, sc), 0, 0)),
            pl.BlockSpec(memory_space=pl.ANY),
            pl.BlockSpec((None, 1, tn), lambda j, t, te, nx, od, sc: (expert(t, te, sc), 0, j)),
            pl.BlockSpec((None, 1, tn),
                         lambda j, t, te, nx, od, sc: (expert(t, te, sc), 0, nj + j)),
        ],
        out_specs=pl.BlockSpec((tm, tn), lambda j, t, te, nx, od, sc: (t, j)),
        scratch_shapes=[pltpu.VMEM((2, d, tn), F32), pltpu.VMEM((2, d, tn), F32),
                        *_tile_scratch(tm), pltpu.SemaphoreType.DMA((2,))],
    )
    return pl.pallas_call(
        _gate_up_kernel,
        grid_spec=grid_spec,
        out_shape=jax.ShapeDtypeStruct((nt * tm, f), BF16),
        compiler_params=_params(("arbitrary", "arbitrary")),
        name="gate_up",
    )(*tiles, xs, w_gate_up, b_gate_up3, b_gate_up3)


def _down_kernel(te_ref, nxt_ref, ord_ref, sc_ref, a_ref, w_ref, bd_ref, y_ref, wd_buf,
                 *scratch):
    tile_scratch, sem = scratch[:-1], scratch[-1]
    t = pl.program_id(0)
    used = t < sc_ref[0]

    def fetch(e, jj, slot):
        return [pltpu.make_async_copy(w_ref.at[e], wd_buf.at[slot], sem.at[slot])]

    @pl.when(used)
    def _():
        slot = _stream_weights(t, 0, 1, te_ref, nxt_ref, ord_ref, sc_ref, fetch)
        y = jnp.dot(a_ref[...].astype(F32), wd_buf[slot],
                    preferred_element_type=F32) + bd_ref[...]
        _store_row_tiles(y_ref, y, tile_scratch)

    @pl.when(jnp.logical_not(used))
    def _():
        y_ref[...] = jnp.zeros_like(y_ref)


def _down(tiles, act, w_down, b_down3):
    n_rows, f = act.shape
    ne, _, d = w_down.shape
    tm = MOE_TM
    nt = n_rows // tm
    grid_spec = pltpu.PrefetchScalarGridSpec(
        num_scalar_prefetch=4,
        grid=(nt,),
        in_specs=[
            pl.BlockSpec((tm, f), lambda t, te, nx, od, sc: (_used_tile(t, sc), 0)),
            pl.BlockSpec(memory_space=pl.ANY),
            pl.BlockSpec((None, 1, d), lambda t, te, nx, od, sc: (te[_used_tile(t, sc)], 0, 0)),
        ],
        out_specs=pl.BlockSpec((tm, ROW_TILE, LANES), lambda t, te, nx, od, sc: (t, 0, 0)),
        scratch_shapes=[pltpu.VMEM((2, f, d), F32), *_tile_scratch(tm),
                        pltpu.SemaphoreType.DMA((2,))],
    )
    return pl.pallas_call(
        _down_kernel,
        grid_spec=grid_spec,
        out_shape=jax.ShapeDtypeStruct((n_rows, ROW_TILE, LANES), BF16),
        compiler_params=_params(("arbitrary",)),
        name="down",
    )(*tiles, act, w_down, b_down3)


def _combine_kernel(dest_ref, dest_next_ref, p_ref, x1_ref, gpost_ref, g2_ref, y_ref, o_ref,
                    *scratch):
    tt = x1_ref.shape[0]
    bufs = (scratch[:TOP_K], scratch[TOP_K:2 * TOP_K])
    tile_scratch, sem = scratch[2 * TOP_K:-1], scratch[-1]
    step = pl.program_id(0)

    def rows(dref, slot, fn):
        def body(r, carry):
            for k in range(TOP_K):
                fn(pltpu.make_async_copy(y_ref.at[dref[r * TOP_K + k]],
                                         bufs[slot][k].at[r], sem.at[slot]), k)
            return carry
        lax.fori_loop(0, tt, body, 0, unroll=ROW_UNROLL)

    for slot in range(2):
        @pl.when(lax.rem(step, 2) == slot)
        def _(slot=slot):
            if slot == 0:
                @pl.when(step == 0)
                def _():
                    rows(dest_ref, 0, lambda cp, k: cp.start(priority=k % 2))

            @pl.when(step + 1 < pl.num_programs(0))
            def _():
                rows(dest_next_ref, 1 - slot, lambda cp, k: cp.start(priority=k % 2))

            rows(dest_ref, slot, lambda cp, k: cp.wait())
            def weighted(r, carry):
                for k in range(TOP_K):
                    term = p_ref[r * TOP_K + k] * bufs[slot][k][r].astype(F32)
                    tile = term if k == 0 else tile + term
                at = pl.ds(pl.multiple_of(r * SUBLANES, SUBLANES), SUBLANES)
                for h, sc in enumerate(tile_scratch):
                    sc[at, :] = tile[h * SUBLANES:(h + 1) * SUBLANES, :]
                return carry

            lax.fori_loop(0, tt, weighted, 0, unroll=ROW_UNROLL)
            y = _rows_from_scratch(tile_scratch, tt)
            o_ref[...] = x1_ref[...] + g2_ref[...] * (_rms(y, RMS_EPS) * gpost_ref[...])


def _combine(dest_flat, probs, x1, gpost, g2, y_tiles, seq):
    n, d = x1.shape
    tt = ROW_TT
    last = n // tt - 1
    return pl.pallas_call(
        _combine_kernel,
        grid=(n // tt,),
        in_specs=[pl.BlockSpec((tt * TOP_K,), lambda i: (i,), memory_space=pltpu.SMEM),
                  pl.BlockSpec((tt * TOP_K,), lambda i: (jnp.minimum(i + 1, last),),
                               memory_space=pltpu.SMEM),
                  pl.BlockSpec((tt * TOP_K,), lambda i: (i,), memory_space=pltpu.SMEM),
                  pl.BlockSpec((tt, d), lambda i: (i, 0)),
                  pl.BlockSpec((1, d), lambda i: (0, 0)),
                  pl.BlockSpec((None, 1, d), lambda i: ((i * tt) // seq, 0, 0)),
                  pl.BlockSpec(memory_space=pl.ANY)],
        out_specs=pl.BlockSpec((tt, d), lambda i: (i, 0)),
        out_shape=jax.ShapeDtypeStruct((n, d), F32),
        scratch_shapes=[pltpu.VMEM((tt, ROW_TILE, LANES), BF16)] * (2 * TOP_K)
                       + _tile_scratch(tt) + [pltpu.SemaphoreType.DMA((2,))],
        compiler_params=_params(("arbitrary",)),
        name="combine",
    )(dest_flat, dest_flat, probs.reshape(n * TOP_K), x1, gpost, g2, y_tiles)


def _layer(x2, c_pad, batch, seq, norm_mix_pre, norm_mix_post, norm_ffn_pre, norm_ffn_post,
           w_ada, b_ada, w_in, rel_bias, w_attn_out, w_dw, b_dw, conv_ln_g, conv_ln_b,
           w_conv_out, w_out, w_router, b_router, w_gate_up, b_gate_up, w_down, b_down):
    n, d = x2.shape
    attn_width = w_attn_out.shape[0]
    conv_width = w_conv_out.shape[0]
    taps = w_dw.shape[0]
    max_rel = (rel_bias.shape[1] - 1) // 2
    ne = w_router.shape[1]
    assert d == ROW_TILE * LANES, "row tiles assume 16 * 128 features per token"
    row = lambda v: v.reshape(1, -1)

    mod = _ada(c_pad, w_ada, row(b_ada))[:batch]
    sh1, sc1, g1, sh2, sc2, g2 = [m.reshape(batch, 1, d) for m in jnp.split(mod, 6, axis=-1)]

    proj = _inproj(x2, row(norm_mix_pre), sc1, sh1, w_in.astype(BF16), seq)

    g, masks = _attn_tables(rel_bias, ATTN_TQ, LEFT_CHUNKS * CHUNK, max_rel)
    attn = _attention(proj, g, masks, batch, seq, attn_width, 0)

    col_a = 3 * attn_width // conv_width
    w_dw_pad = jnp.pad(w_dw, ((0, CONV_HALO - taps), (0, 0)))
    cv = _conv(proj, w_dw_pad, row(b_dw), row(conv_ln_g), row(conv_ln_b),
               batch, seq, col_a, col_a + 1, conv_width, taps)

    col_g = (3 * attn_width + 2 * conv_width) // (d // 2)
    x1, h_tiles, logits = _mix(
        attn, cv, proj, x2, w_attn_out.astype(BF16), w_conv_out.astype(BF16),
        w_out.astype(BF16), row(norm_mix_post), g1, row(norm_ffn_pre), sc2, sh2,
        w_router.astype(BF16), row(b_router), seq, col_g)

    idx, probs, counts_f = _topk(logits)
    counts = counts_f.reshape(ne).astype(I32)
    padded = ((counts + MOE_TM - 1) // MOE_TM) * MOE_TM
    ends = jnp.cumsum(padded)
    starts = ends - padded
    n_rows = n * TOP_K + ne * MOE_TM
    n_tiles = n_rows // MOE_TM
    tile_start = jnp.arange(n_tiles, dtype=I32) * MOE_TM
    tile_e = jnp.minimum(jnp.sum((tile_start[:, None] >= ends[None, :]).astype(I32), axis=1),
                         ne - 1)
    n_used = (ends[-1:] // MOE_TM).astype(I32)
    experts = jnp.arange(ne, dtype=I32)
    nonempty = counts > 0
    later = nonempty[None, :] & (experts[None, :] > experts[:, None])
    next_e = jnp.min(jnp.where(later, experts[None, :], ne), axis=1)
    next_e = jnp.where(next_e < ne, next_e, jnp.min(jnp.where(nonempty, experts, ne)))
    ordinal_e = jnp.cumsum(nonempty.astype(I32)) - 1
    n_groups = jnp.sum(nonempty.astype(I32), keepdims=True)
    tiles = (tile_e, next_e[tile_e].astype(I32), ordinal_e[tile_e].astype(I32),
             jnp.concatenate([n_used, n_groups]))

    dest = _slots(idx, starts.astype(F32).reshape(1, ne))
    dest_flat = dest.reshape(n * TOP_K)
    xs = _scatter_rows(dest_flat, counts, starts, n_used, h_tiles, n_rows)
    act = _gate_up(tiles, xs, w_gate_up, b_gate_up.reshape(ne, 1, -1))
    y_tiles = _down(tiles, act, w_down, b_down.reshape(ne, 1, -1))
    return _combine(dest_flat, probs, x1, row(norm_ffn_post), g2, y_tiles, seq)


def kernel(x, c, norm_mix_pre, norm_mix_post, norm_ffn_pre, norm_ffn_post, w_ada, b_ada, w_in,
           rel_bias, w_attn_out, w_dw, b_dw, conv_ln_g, conv_ln_b, w_conv_out, w_out, w_router,
           b_router, w_gate_up, b_gate_up, w_down, b_down):
    batch, seq, d = x.shape
    x2 = x.reshape(batch * seq, d)
    c_pad = jnp.pad(c, ((0, SUBLANES - batch % SUBLANES), (0, 0))) if batch % SUBLANES else c
    for l in range(w_ada.shape[0]):
        x2 = _layer(x2, c_pad, batch, seq, norm_mix_pre[l], norm_mix_post[l], norm_ffn_pre[l],
                    norm_ffn_post[l], w_ada[l], b_ada[l], w_in[l], rel_bias[l], w_attn_out[l],
                    w_dw[l], b_dw[l], conv_ln_g[l], conv_ln_b[l], w_conv_out[l], w_out[l],
                    w_router[l], b_router[l], w_gate_up[l], b_gate_up[l], w_down[l], b_down[l])
    return x2.reshape(batch, seq, d)
```

```python
import functools

import numpy as np

import jax
import jax.numpy as jnp
from jax import lax
from jax.experimental import pallas as pl
from jax.experimental.pallas import tpu as pltpu

F32 = jnp.float32
BF16 = jnp.bfloat16
I32 = jnp.int32

CHUNK = 64
LEFT_CHUNKS = 8
HEAD_DIM = 64
TOP_K = 4
SWIGLU_LIMIT = 7.0
SWIGLU_ALPHA = 1.702
RMS_EPS = 1e-6
LN_EPS = 1e-5
MASK_VALUE = -1e30

LANES = 128
SUBLANES = 8
ROW_TILE = 16
VMEM_LIMIT_BYTES = 56 * 1024 * 1024

ADA_TN = 1024
INPROJ_TM = 1024
INPROJ_TN = 1024
INPROJ_CHUNKS = 8
ATTN_TQ = 256
CONV_TS = 128
CONV_CG = 256
CONV_HALO = 32
MIX_TM = 256
TOPK_TM = 1024
SLOT_TM = 512
MOE_TM = 256
GATE_UP_TN = 1024
ROW_TT = 256
ROW_UNROLL = 4
WEIGHT_DMA_PRIORITY = 1


def _params(sem, vmem=VMEM_LIMIT_BYTES):
    return pltpu.CompilerParams(dimension_semantics=sem, vmem_limit_bytes=vmem)


def _rms(x, eps):
    return x * lax.rsqrt(jnp.mean(x * x, axis=-1, keepdims=True) + eps)


def _store_row_tiles(ref, value, scratch):
    rows = value.shape[0]
    for s in range(ROW_TILE):
        scratch[s // SUBLANES][pl.ds(s % SUBLANES, rows, stride=SUBLANES), :] = (
            value[:, s * LANES:(s + 1) * LANES])
    halves = [sc[...].reshape(rows, SUBLANES, LANES) for sc in scratch]
    ref[...] = jnp.concatenate(halves, axis=1).astype(BF16)


def _tiles_to_rows(tiles, scratch):
    rows = tiles.shape[0]
    for h, sc in enumerate(scratch):
        sc[...] = tiles[:, h * SUBLANES:(h + 1) * SUBLANES, :].reshape(rows * SUBLANES, LANES)
    return _rows_from_scratch(scratch, rows)


def _rows_from_scratch(scratch, rows):
    return jnp.concatenate(
        [sc[pl.ds(s, rows, stride=SUBLANES), :] for sc in scratch for s in range(SUBLANES)],
        axis=1)


def _tile_scratch(rows):
    return [pltpu.VMEM((rows * SUBLANES, LANES), F32)] * (ROW_TILE // SUBLANES)


def _ada_kernel(c_ref, w_ref, b_ref, o_ref):
    c = c_ref[...]
    s = c * jax.nn.sigmoid(c)
    o_ref[...] = jnp.dot(s.astype(BF16), w_ref[...].astype(BF16),
                         preferred_element_type=F32) + b_ref[...]


def _ada(c_pad, w_ada, b_ada):
    rows, d = c_pad.shape
    n_out = w_ada.shape[1]
    return pl.pallas_call(
        _ada_kernel,
        grid=(n_out // ADA_TN,),
        in_specs=[pl.BlockSpec((rows, d), lambda j: (0, 0)),
                  pl.BlockSpec((d, ADA_TN), lambda j: (0, j)),
                  pl.BlockSpec((1, ADA_TN), lambda j: (0, j))],
        out_specs=pl.BlockSpec((rows, ADA_TN), lambda j: (0, j)),
        out_shape=jax.ShapeDtypeStruct((rows, n_out), F32),
        compiler_params=_params(("arbitrary",)),
        name="ada",
    )(c_pad, w_ada, b_ada)


def _inproj_kernel(x0_ref, xn_ref, g_ref, sc0_ref, sh0_ref, scn_ref, shn_ref, w_ref, o_ref,
                   h_ref):
    i = pl.program_id(0)
    j = pl.program_id(1)
    tm = o_ref.shape[0]
    chunk = tm // INPROJ_CHUNKS

    def modulated(x, sc, sh):
        return ((_rms(x, RMS_EPS) * g_ref[...]) * (1.0 + sc) + sh).astype(BF16)

    @pl.when(jnp.logical_and(i == 0, j == 0))
    def _():
        h_ref[0] = modulated(x0_ref[...], sc0_ref[...], sh0_ref[...])

    c = jnp.minimum(j, INPROJ_CHUNKS - 1)
    rows = pl.ds(pl.multiple_of(c * chunk, chunk), chunk)
    h_ref[lax.rem(i + 1, 2), rows, :] = modulated(xn_ref[rows, :], scn_ref[...], shn_ref[...])
    o_ref[...] = jnp.dot(h_ref[lax.rem(i, 2)], w_ref[...],
                         preferred_element_type=F32).astype(BF16)


def _inproj(x2, g, sc, sh, w_bf16, seq):
    n, d = x2.shape
    width = w_bf16.shape[1]
    tm, tn = INPROJ_TM, INPROJ_TN
    assert width // tn >= INPROJ_CHUNKS
    nxt = lambda i: jnp.minimum(i + 1, n // tm - 1)
    first_mod = pl.BlockSpec((None, 1, d), lambda i, j: (0, 0, 0))
    next_mod = pl.BlockSpec((None, 1, d), lambda i, j: ((nxt(i) * tm) // seq, 0, 0))
    return pl.pallas_call(
        _inproj_kernel,
        grid=(n // tm, width // tn),
        in_specs=[pl.BlockSpec((tm, d), lambda i, j: (0, 0), pipeline_mode=pl.Buffered(1)),
                  pl.BlockSpec((tm, d), lambda i, j: (nxt(i), 0)),
                  pl.BlockSpec((1, d), lambda i, j: (0, 0)),
                  first_mod, first_mod, next_mod, next_mod,
                  pl.BlockSpec((d, tn), lambda i, j: (0, j))],
        out_specs=pl.BlockSpec((tm, tn), lambda i, j: (i, j)),
        out_shape=jax.ShapeDtypeStruct((n, width), BF16),
        scratch_shapes=[pltpu.VMEM((2, tm, d), BF16)],
        compiler_params=_params(("arbitrary", "arbitrary")),
        name="inproj",
    )(x2, x2, g, sc, sh, sc, sh, w_bf16)


def _attn_kernel(*refs, n_blocks):
    q_ref = refs[0]
    k_refs = refs[1:1 + n_blocks]
    v_refs = refs[1 + n_blocks:1 + 2 * n_blocks]
    g_ref, mask_ref, o_ref, bias_ref = refs[1 + 2 * n_blocks:]
    heads = g_ref.shape[0]
    tq = q_ref.shape[0]
    win = mask_ref.shape[-1]

    @pl.when(jnp.logical_and(pl.program_id(0) == 0, pl.program_id(1) == 0))
    def _():
        for h in range(heads):
            rows = jnp.broadcast_to(g_ref[h], (tq, g_ref.shape[-1]))
            bias_ref[h] = pltpu.roll(rows, 0, 1, stride=1, stride_axis=0)[:, :win]

    scale = HEAD_DIM ** -0.5
    per_vreg = LANES // HEAD_DIM
    for hp in range(heads // per_vreg):
        cols = slice(hp * LANES, (hp + 1) * LANES)
        q = q_ref[:, cols]
        k = jnp.concatenate([r[:, cols] for r in k_refs], axis=0)
        v = jnp.concatenate([r[:, cols] for r in v_refs], axis=0)
        outs = []
        for hh in range(per_vreg):
            sl = slice(hh * HEAD_DIM, (hh + 1) * HEAD_DIM)
            s = lax.dot_general(q[:, sl], k[:, sl], (((1,), (1,)), ((), ())),
                                preferred_element_type=F32)
            s = s * scale + bias_ref[hp * per_vreg + hh] + mask_ref[...]
            m = jnp.max(s, axis=-1, keepdims=True)
            p = jnp.exp(s - m).astype(BF16)
            lane = lax.broadcasted_iota(I32, v.shape, 1)
            own = (lane >= hh * HEAD_DIM) & (lane < (hh + 1) * HEAD_DIM)
            o = jnp.dot(p, jnp.where(own, v, jnp.ones_like(v)), preferred_element_type=F32)
            other = slice((1 - hh) * HEAD_DIM, (2 - hh) * HEAD_DIM)
            outs.append(o[:, sl] / o[:, other])
        o_ref[:, cols] = jnp.concatenate(outs, axis=1).astype(BF16)


def _attn_tables(rel_bias, tq, left, max_rel):
    win = left + tq
    length = pl.next_power_of_2(win + tq)
    u = np.arange(length)
    signed = np.where(u < win, u, u - length)
    idx = np.clip(left - signed, -max_rel, max_rel) + max_rel
    g = rel_bias[:, idx].astype(F32).reshape(rel_bias.shape[0], 1, length)

    r = np.arange(tq)[:, None]
    c = np.arange(win)[None, :]
    band = (c // CHUNK >= r // CHUNK) & (c // CHUNK <= r // CHUNK + LEFT_CHUNKS)
    masks = [np.where(band & (c >= left - var * tq), 0.0, MASK_VALUE)
             for var in range(left // tq + 1)]
    return g, jnp.asarray(np.stack(masks), F32)


def _attention(proj, g, masks, batch, seq, attn_width, q_off):
    n = proj.shape[0]
    tq = ATTN_TQ
    nq = seq // tq
    heads = g.shape[0]
    win = masks.shape[-1]
    nvar = masks.shape[0]
    nback = nvar - 1
    qcol = q_off // attn_width
    blk = (tq, attn_width)

    def kv_spec(col, back):
        return pl.BlockSpec(blk, lambda b, qi: (b * nq + jnp.maximum(qi - back, 0), col))

    in_specs = [pl.BlockSpec(blk, lambda b, qi: (b * nq + qi, qcol))]
    in_specs += [kv_spec(qcol + 1, back) for back in range(nback, -1, -1)]
    in_specs += [kv_spec(qcol + 2, back) for back in range(nback, -1, -1)]
    in_specs += [pl.BlockSpec(g.shape, lambda b, qi: (0, 0, 0)),
                 pl.BlockSpec((None, tq, win), lambda b, qi: (jnp.minimum(qi, nvar - 1), 0, 0))]
    return pl.pallas_call(
        functools.partial(_attn_kernel, n_blocks=nback + 1),
        grid=(batch, nq),
        in_specs=in_specs,
        out_specs=pl.BlockSpec(blk, lambda b, qi: (b * nq + qi, 0)),
        out_shape=jax.ShapeDtypeStruct((n, attn_width), BF16),
        scratch_shapes=[pltpu.VMEM((heads, tq, win), F32)],
        compiler_params=_params(("arbitrary", "arbitrary")),
        name="attn",
    )(proj, *([proj] * (2 * (nback + 1))), g, masks)


def _conv_kernel(ca_ref, cb_ref, w_ref, b_ref, g_ref, beta_ref, o_ref, ubuf, ush, cbuf, *, taps):
    ts = ca_ref.shape[0]
    width = ca_ref.shape[1]
    halo = CONV_HALO
    first = pl.program_id(1) == 0

    @pl.when(first)
    def _():
        ubuf[0:halo, :] = jnp.zeros((halo, width), F32)

    @pl.when(jnp.logical_not(first))
    def _():
        ubuf[0:halo, :] = ubuf[ts:ts + halo, :]

    ubuf[halo:halo + ts, :] = (ca_ref[...].astype(F32)
                               * jax.nn.sigmoid(cb_ref[...].astype(F32)))
    span = ush.shape[1]
    for s in range(1, SUBLANES):
        ush[s - 1] = ubuf[s:s + span, :]
    off = halo - (taps - 1)
    for cg in range(width // CONV_CG):
        cs = slice(cg * CONV_CG, (cg + 1) * CONV_CG)
        acc = jnp.zeros((ts, CONV_CG), F32)
        for j in range(taps):
            base, s = divmod(off + j, SUBLANES)
            base *= SUBLANES
            src = ubuf[base:base + ts, cs] if s == 0 else ush[s - 1, base:base + ts, cs]
            acc = acc + w_ref[j:j + 1, cs] * src
        cbuf[:, cs] = acc + b_ref[:, cs]
    y = cbuf[...]
    mu = jnp.mean(y, axis=-1, keepdims=True)
    yc = y - mu
    var = jnp.mean(yc * yc, axis=-1, keepdims=True)
    yn = yc * lax.rsqrt(var + LN_EPS) * g_ref[...] + beta_ref[...]
    o_ref[...] = (yn * jax.nn.sigmoid(yn)).astype(BF16)


def _conv(proj, w_dw_pad, b_dw, ln_g, ln_b, batch, seq, col_a, col_b, width, taps):
    n = proj.shape[0]
    ts = CONV_TS
    ns = seq // ts
    vec = pl.BlockSpec((1, width), lambda b, i: (0, 0))
    return pl.pallas_call(
        functools.partial(_conv_kernel, taps=taps),
        grid=(batch, ns),
        in_specs=[pl.BlockSpec((ts, width), lambda b, i: (b * ns + i, col_a)),
                  pl.BlockSpec((ts, width), lambda b, i: (b * ns + i, col_b)),
                  pl.BlockSpec(w_dw_pad.shape, lambda b, i: (0, 0)),
                  vec, vec, vec],
        out_specs=pl.BlockSpec((ts, width), lambda b, i: (b * ns + i, 0)),
        out_shape=jax.ShapeDtypeStruct((n, width), BF16),
        scratch_shapes=[pltpu.VMEM((CONV_HALO + ts, width), F32),
                        pltpu.VMEM((SUBLANES - 1, CONV_HALO + ts - SUBLANES, width), F32),
                        pltpu.VMEM((ts, width), F32)],
        compiler_params=_params(("arbitrary", "arbitrary")),
        name="conv",
    )(proj, proj, w_dw_pad, b_dw, ln_g, ln_b)


def _mix_kernel(attn_ref, cv_ref, ga0_ref, ga1_ref, gc0_ref, gc1_ref, x_ref, wa_ref, wc_ref,
                wo_ref, gpost_ref, g1_ref, gpre_ref, sc_ref, sh_ref, wr_ref, br_ref,
                x1_ref, hp_ref, lg_ref, *tile_scratch):
    ya = jnp.dot(attn_ref[...], wa_ref[...], preferred_element_type=F32)
    yc = jnp.dot(cv_ref[...], wc_ref[...], preferred_element_type=F32)
    ga = jnp.concatenate([ga0_ref[...], ga1_ref[...]], axis=1).astype(F32)
    gc = jnp.concatenate([gc0_ref[...], gc1_ref[...]], axis=1).astype(F32)
    merged = jax.nn.sigmoid(ga) * ya + jax.nn.sigmoid(gc) * yc
    y = jnp.dot(merged.astype(BF16), wo_ref[...], preferred_element_type=F32)
    x1 = x_ref[...] + g1_ref[...] * (_rms(y, RMS_EPS) * gpost_ref[...])
    x1_ref[...] = x1
    h = _rms(x1, RMS_EPS) * gpre_ref[...] * (1.0 + sc_ref[...]) + sh_ref[...]
    _store_row_tiles(hp_ref, h, tile_scratch)
    lg_ref[...] = jnp.dot(h.astype(BF16), wr_ref[...], preferred_element_type=F32) + br_ref[...]


def _mix(attn, cv, proj, x2, wa, wc, wo, gpost, g1, gpre, sc, sh, wr, br, seq, col_g):
    n, d = x2.shape
    tm = MIX_TM
    aw = attn.shape[1]
    cw = cv.shape[1]
    ne = wr.shape[1]
    gw = d // 2
    const = lambda shape: pl.BlockSpec(shape, lambda i: (0, 0), pipeline_mode=pl.Buffered(1))
    mod_spec = pl.BlockSpec((None, 1, d), lambda i: ((i * tm) // seq, 0, 0))
    gate_spec = lambda col: pl.BlockSpec((tm, gw), lambda i: (i, col))
    return pl.pallas_call(
        _mix_kernel,
        grid=(n // tm,),
        in_specs=[pl.BlockSpec((tm, aw), lambda i: (i, 0)),
                  pl.BlockSpec((tm, cw), lambda i: (i, 0)),
                  gate_spec(col_g), gate_spec(col_g + 1),
                  gate_spec(col_g + 2), gate_spec(col_g + 3),
                  pl.BlockSpec((tm, d), lambda i: (i, 0)),
                  const((aw, d)), const((cw, d)), const((d, d)),
                  const((1, d)), mod_spec, const((1, d)), mod_spec, mod_spec,
                  const((d, ne)), const((1, ne))],
        out_specs=[pl.BlockSpec((tm, d), lambda i: (i, 0)),
                   pl.BlockSpec((tm, ROW_TILE, LANES), lambda i: (i, 0, 0)),
                   pl.BlockSpec((tm, ne), lambda i: (i, 0))],
        out_shape=[jax.ShapeDtypeStruct((n, d), F32),
                   jax.ShapeDtypeStruct((n, ROW_TILE, LANES), BF16),
                   jax.ShapeDtypeStruct((n, ne), F32)],
        scratch_shapes=_tile_scratch(tm),
        compiler_params=_params(("arbitrary",)),
        name="mix",
    )(attn, cv, proj, proj, proj, proj, x2, wa, wc, wo, gpost, g1, gpre, sc, sh, wr, br)


def _columns_to_lanes(cols, dtype):
    tm = cols[0].shape[0]
    lane = lax.broadcasted_iota(I32, (tm, len(cols)), 1)
    out = jnp.zeros((tm, len(cols)), dtype)
    for k, col in enumerate(cols):
        out = jnp.where(lane == k, col, out)
    return out


def _topk_kernel(lg_ref, idx_ref, p_ref, cnt_ref):
    logits = lg_ref[...]
    tm, ne = logits.shape
    lane = lax.broadcasted_iota(I32, (tm, ne), 1)
    work = logits
    vals, idxs = [], []
    chosen = jnp.zeros((tm, ne), F32)
    for _ in range(TOP_K):
        m = jnp.max(work, axis=-1, keepdims=True)
        idx = jnp.min(jnp.where(work == m, lane, ne), axis=-1, keepdims=True)
        sel = lane == idx
        vals.append(m)
        idxs.append(idx)
        chosen = chosen + sel.astype(F32)
        work = jnp.where(sel, -jnp.inf, work)
    exps = [jnp.exp(v - vals[0]) for v in vals]
    denom = exps[0]
    for e in exps[1:]:
        denom = denom + e
    idx_ref[...] = _columns_to_lanes(idxs, I32)
    p_ref[...] = _columns_to_lanes([e / denom for e in exps], F32)

    @pl.when(pl.program_id(0) == 0)
    def _():
        cnt_ref[...] = jnp.zeros_like(cnt_ref)

    cnt_ref[...] += jnp.sum(chosen, axis=0, keepdims=True)


def _topk(logits):
    n, ne = logits.shape
    tm = TOPK_TM
    return pl.pallas_call(
        _topk_kernel,
        grid=(n // tm,),
        in_specs=[pl.BlockSpec((tm, ne), lambda i: (i, 0))],
        out_specs=[pl.BlockSpec((tm, TOP_K), lambda i: (i, 0)),
                   pl.BlockSpec((tm, TOP_K), lambda i: (i, 0)),
                   pl.BlockSpec((1, ne), lambda i: (0, 0))],
        out_shape=[jax.ShapeDtypeStruct((n, TOP_K), I32),
                   jax.ShapeDtypeStruct((n, TOP_K), F32),
                   jax.ShapeDtypeStruct((1, ne), F32)],
        compiler_params=_params(("arbitrary",)),
        name="topk",
    )(logits)


def _slots_kernel(idx_ref, start_ref, dest_ref, carry_ref):
    @pl.when(pl.program_id(0) == 0)
    def _():
        carry_ref[...] = jnp.zeros_like(carry_ref)

    idx = idx_ref[...]
    tm = idx.shape[0]
    ne = start_ref.shape[1]
    lane = lax.broadcasted_iota(I32, (tm, ne), 1)
    sels = [lane == idx[:, k:k + 1] for k in range(TOP_K)]
    chosen = jnp.zeros((tm, ne), F32)
    for sel in sels:
        chosen = chosen + sel.astype(F32)
    row = lax.broadcasted_iota(I32, (tm, tm), 0)
    col = lax.broadcasted_iota(I32, (tm, tm), 1)
    earlier = (col < row).astype(BF16)
    rank = jnp.dot(earlier, chosen.astype(BF16), preferred_element_type=F32)
    slot = start_ref[...] + carry_ref[...] + rank
    dests = [jnp.sum(jnp.where(sel, slot, 0.0), axis=-1, keepdims=True).astype(I32)
             for sel in sels]
    dest_ref[...] = _columns_to_lanes(dests, I32)
    carry_ref[...] += jnp.sum(chosen, axis=0, keepdims=True)


def _slots(idx, start_f32):
    n = idx.shape[0]
    ne = start_f32.shape[1]
    tm = SLOT_TM
    return pl.pallas_call(
        _slots_kernel,
        grid=(n // tm,),
        in_specs=[pl.BlockSpec((tm, TOP_K), lambda i: (i, 0)),
                  pl.BlockSpec((1, ne), lambda i: (0, 0))],
        out_specs=pl.BlockSpec((tm, TOP_K), lambda i: (i, 0)),
        out_shape=jax.ShapeDtypeStruct((n, TOP_K), I32),
        scratch_shapes=[pltpu.VMEM((1, ne), F32)],
        compiler_params=_params(("arbitrary",)),
        name="slots",
    )(idx, start_f32)


def _scatter_kernel(dest_ref, cnt_ref, start_ref, nu_ref, h_ref, xs_ref, zero_ref, sem, zsem,
                    *, ne, n_tiles):
    tt = h_ref.shape[0]

    def row_copy(r, k):
        return pltpu.make_async_copy(h_ref.at[r], xs_ref.at[dest_ref[r * TOP_K + k]], sem)

    def start_rows(r, carry):
        for k in range(TOP_K):
            row_copy(r, k).start(priority=k % 2)
        return carry

    def wait_rows(r, carry):
        for k in range(TOP_K):
            row_copy(r, k).wait()
        return carry

    lax.fori_loop(0, tt, start_rows, 0, unroll=ROW_UNROLL)

    @pl.when(pl.program_id(0) == 0)
    def _():
        zero_ref[...] = jnp.zeros_like(zero_ref)

        def pad_copy(s):
            return pltpu.make_async_copy(zero_ref.at[0], xs_ref.at[s], zsem)

        def tile_copy(t):
            dst = xs_ref.at[pl.ds(pl.multiple_of(t * MOE_TM, MOE_TM), MOE_TM)]
            return pltpu.make_async_copy(zero_ref, dst, zsem)

        def loop(lo, hi, fn):
            def body(i, c):
                fn(i)
                return c
            lax.fori_loop(lo, hi, body, 0)

        def per_expert(e, carry):
            cnt = cnt_ref[e]
            lo = start_ref[e] + cnt
            hi = start_ref[e] + ((cnt + MOE_TM - 1) // MOE_TM) * MOE_TM
            loop(lo, hi, lambda s: pad_copy(s).start())
            loop(lo, hi, lambda s: pad_copy(s).wait())
            return carry

        lax.fori_loop(0, ne, per_expert, 0)
        loop(nu_ref[0], n_tiles, lambda t: tile_copy(t).start())
        loop(nu_ref[0], n_tiles, lambda t: tile_copy(t).wait())

    lax.fori_loop(0, tt, wait_rows, 0, unroll=ROW_UNROLL)


def _scatter_rows(dest_flat, counts, starts, n_used, h_tiles, n_rows):
    ne = counts.shape[0]
    tt = ROW_TT
    n = h_tiles.shape[0]
    smem = pl.BlockSpec(memory_space=pltpu.SMEM)
    return pl.pallas_call(
        functools.partial(_scatter_kernel, ne=ne, n_tiles=n_rows // MOE_TM),
        grid=(n // tt,),
        in_specs=[pl.BlockSpec((tt * TOP_K,), lambda i: (i,), memory_space=pltpu.SMEM),
                  smem, smem, smem,
                  pl.BlockSpec((tt, ROW_TILE, LANES), lambda i: (i, 0, 0))],
        out_specs=pl.BlockSpec(memory_space=pl.ANY),
        out_shape=jax.ShapeDtypeStruct((n_rows, ROW_TILE, LANES), BF16),
        scratch_shapes=[pltpu.VMEM((MOE_TM, ROW_TILE, LANES), BF16),
                        pltpu.SemaphoreType.DMA, pltpu.SemaphoreType.DMA],
        compiler_params=_params(("arbitrary",)),
        name="scatter_rows",
    )(dest_flat, counts, starts, n_used, h_tiles)


def _new_expert(te_ref, t):
    prev = te_ref[jnp.maximum(t - 1, 0)]
    return jnp.logical_or(t == 0, te_ref[t] != prev)


def _stream_weights(t, pass_idx, n_pass, te_ref, nxt_ref, ord_ref, sc_ref, fetch):
    n_groups = sc_ref[1]
    ordinal = ord_ref[t]
    group = ordinal + pass_idx * n_groups
    slot = lax.rem(group, 2)

    @pl.when(_new_expert(te_ref, t))
    def _():
        @pl.when(group == 0)
        def _():
            for cp in fetch(te_ref[t], pass_idx, slot):
                cp.start(priority=WEIGHT_DMA_PRIORITY)

        for cp in fetch(te_ref[t], pass_idx, slot):
            cp.wait()
        last = ordinal == n_groups - 1

        @pl.when(jnp.logical_not(jnp.logical_and(last, pass_idx == n_pass - 1)))
        def _():
            for cp in fetch(nxt_ref[t], jnp.where(last, pass_idx + 1, pass_idx), 1 - slot):
                cp.start(priority=WEIGHT_DMA_PRIORITY)

    return slot


def _gate_up_kernel(te_ref, nxt_ref, ord_ref, sc_ref, xs_ref, w_ref, bg_ref, bu_ref, a_ref,
                    wg_buf, wu_buf, *scratch):
    tile_scratch, sem = scratch[:-1], scratch[-1]
    t = pl.program_id(1)
    used = t < sc_ref[0]
    tn = a_ref.shape[1]
    f = w_ref.shape[2] // 2

    def fetch(e, jj, slot):
        col = pl.multiple_of(jj * tn, tn)
        return [pltpu.make_async_copy(w_ref.at[e, :, pl.ds(col, tn)], wg_buf.at[slot],
                                      sem.at[slot]),
                pltpu.make_async_copy(w_ref.at[e, :, pl.ds(f + col, tn)], wu_buf.at[slot],
                                      sem.at[slot])]

    @pl.when(used)
    def _():
        slot = _stream_weights(t, pl.program_id(0), pl.num_programs(0), te_ref, nxt_ref,
                               ord_ref, sc_ref, fetch)
        x = _tiles_to_rows(xs_ref[...].astype(F32), tile_scratch)
        gate = jnp.dot(x, wg_buf[slot], preferred_element_type=F32) + bg_ref[...]
        up = jnp.dot(x, wu_buf[slot], preferred_element_type=F32) + bu_ref[...]
        gate = jnp.minimum(gate, SWIGLU_LIMIT)
        up = jnp.clip(up, -SWIGLU_LIMIT, SWIGLU_LIMIT)
        glu = gate * jax.nn.sigmoid(SWIGLU_ALPHA * gate)
        a_ref[...] = ((up + 1.0) * glu).astype(BF16)

    @pl.when(jnp.logical_not(used))
    def _():
        a_ref[...] = jnp.zeros_like(a_ref)


def _used_tile(t, sc):
    return jnp.minimum(t, sc[0] - 1)


def _gate_up(tiles, xs, w_gate_up, b_gate_up3):
    ne, d, two_f = w_gate_up.shape
    f = two_f // 2
    tm, tn = MOE_TM, GATE_UP_TN
    nt = xs.shape[0] // tm
    nj = f // tn
    expert = lambda t, te, sc: te[_used_tile(t, sc)]
    grid_spec = pltpu.PrefetchScalarGridSpec(
        num_scalar_prefetch=4,
        grid=(nj, nt),
        in_specs=[
            pl.BlockSpec((tm, ROW_TILE, LANES),
                         lambda j, t, te, nx, od, sc: (_used_tile(t, sc), 0, 0)),
            pl.BlockSpec(memory_space=pl.ANY),
            pl.BlockSpec((None, 1, tn), lambda j, t, te, nx, od, sc: (expert(t, te, sc), 0, j)),
            pl.BlockSpec((None, 1, tn),
                         lambda j, t, te, nx, od, sc: (expert(t, te, sc), 0, nj + j)),
        ],
        out_specs=pl.BlockSpec((tm, tn), lambda j, t, te, nx, od, sc: (t, j)),
        scratch_shapes=[pltpu.VMEM((2, d, tn), F32), pltpu.VMEM((2, d, tn), F32),
                        *_tile_scratch(tm), pltpu.SemaphoreType.DMA((2,))],
    )
    return pl.pallas_call(
        _gate_up_kernel,
        grid_spec=grid_spec,
        out_shape=jax.ShapeDtypeStruct((nt * tm, f), BF16),
        compiler_params=_params(("arbitrary", "arbitrary")),
        name="gate_up",
    )(*tiles, xs, w_gate_up, b_gate_up3, b_gate_up3)


def _down_kernel(te_ref, nxt_ref, ord_ref, sc_ref, a_ref, w_ref, bd_ref, y_ref, wd_buf,
                 *scratch):
    tile_scratch, sem = scratch[:-1], scratch[-1]
    t = pl.program_id(0)
    used = t < sc_ref[0]

    def fetch(e, jj, slot):
        return [pltpu.make_async_copy(w_ref.at[e], wd_buf.at[slot], sem.at[slot])]

    @pl.when(used)
    def _():
        slot = _stream_weights(t, 0, 1, te_ref, nxt_ref, ord_ref, sc_ref, fetch)
        y = jnp.dot(a_ref[...].astype(F32), wd_buf[slot],
                    preferred_element_type=F32) + bd_ref[...]
        _store_row_tiles(y_ref, y, tile_scratch)

    @pl.when(jnp.logical_not(used))
    def _():
        y_ref[...] = jnp.zeros_like(y_ref)


def _down(tiles, act, w_down, b_down3):
    n_rows, f = act.shape
    ne, _, d = w_down.shape
    tm = MOE_TM
    nt = n_rows // tm
    grid_spec = pltpu.PrefetchScalarGridSpec(
        num_scalar_prefetch=4,
        grid=(nt,),
        in_specs=[
            pl.BlockSpec((tm, f), lambda t, te, nx, od, sc: (_used_tile(t, sc), 0)),
            pl.BlockSpec(memory_space=pl.ANY),
            pl.BlockSpec((None, 1, d), lambda t, te, nx, od, sc: (te[_used_tile(t, sc)], 0, 0)),
        ],
        out_specs=pl.BlockSpec((tm, ROW_TILE, LANES), lambda t, te, nx, od, sc: (t, 0, 0)),
        scratch_shapes=[pltpu.VMEM((2, f, d), F32), *_tile_scratch(tm),
                        pltpu.SemaphoreType.DMA((2,))],
    )
    return pl.pallas_call(
        _down_kernel,
        grid_spec=grid_spec,
        out_shape=jax.ShapeDtypeStruct((n_rows, ROW_TILE, LANES), BF16),
        compiler_params=_params(("arbitrary",)),
        name="down",
    )(*tiles, act, w_down, b_down3)


def _combine_kernel(dest_ref, dest_next_ref, p_ref, x1_ref, gpost_ref, g2_ref, y_ref, o_ref,
                    *scratch):
    tt = x1_ref.shape[0]
    bufs = (scratch[:TOP_K], scratch[TOP_K:2 * TOP_K])
    tile_scratch, sem = scratch[2 * TOP_K:-1], scratch[-1]
    step = pl.program_id(0)

    def rows(dref, slot, fn):
        def body(r, carry):
            for k in range(TOP_K):
                fn(pltpu.make_async_copy(y_ref.at[dref[r * TOP_K + k]],
                                         bufs[slot][k].at[r], sem.at[slot]), k)
            return carry
        lax.fori_loop(0, tt, body, 0, unroll=ROW_UNROLL)

    for slot in range(2):
        @pl.when(lax.rem(step, 2) == slot)
        def _(slot=slot):
            if slot == 0:
                @pl.when(step == 0)
                def _():
                    rows(dest_ref, 0, lambda cp, k: cp.start(priority=k % 2))

            @pl.when(step + 1 < pl.num_programs(0))
            def _():
                rows(dest_next_ref, 1 - slot, lambda cp, k: cp.start(priority=k % 2))

            rows(dest_ref, slot, lambda cp, k: cp.wait())
            def weighted(r, carry):
                for k in range(TOP_K):
                    term = p_ref[r * TOP_K + k] * bufs[slot][k][r].astype(F32)
                    tile = term if k == 0 else tile + term
                at = pl.ds(pl.multiple_of(r * SUBLANES, SUBLANES), SUBLANES)
                for h, sc in enumerate(tile_scratch):
                    sc[at, :] = tile[h * SUBLANES:(h + 1) * SUBLANES, :]
                return carry

            lax.fori_loop(0, tt, weighted, 0, unroll=ROW_UNROLL)
            y = _rows_from_scratch(tile_scratch, tt)
            o_ref[...] = x1_ref[...] + g2_ref[...] * (_rms(y, RMS_EPS) * gpost_ref[...])


def _combine(dest_flat, probs, x1, gpost, g2, y_tiles, seq):
    n, d = x1.shape
    tt = ROW_TT
    last = n // tt - 1
    return pl.pallas_call(
        _combine_kernel,
        grid=(n // tt,),
        in_specs=[pl.BlockSpec((tt * TOP_K,), lambda i: (i,), memory_space=pltpu.SMEM),
                  pl.BlockSpec((tt * TOP_K,), lambda i: (jnp.minimum(i + 1, last),),
                               memory_space=pltpu.SMEM),
                  pl.BlockSpec((tt * TOP_K,), lambda i: (i,), memory_space=pltpu.SMEM),
                  pl.BlockSpec((tt, d), lambda i: (i, 0)),
                  pl.BlockSpec((1, d), lambda i: (0, 0)),
                  pl.BlockSpec((None, 1, d), lambda i: ((i * tt) // seq, 0, 0)),
                  pl.BlockSpec(memory_space=pl.ANY)],
        out_specs=pl.BlockSpec((tt, d), lambda i: (i, 0)),
        out_shape=jax.ShapeDtypeStruct((n, d), F32),
        scratch_shapes=[pltpu.VMEM((tt, ROW_TILE, LANES), BF16)] * (2 * TOP_K)
                       + _tile_scratch(tt) + [pltpu.SemaphoreType.DMA((2,))],
        compiler_params=_params(("arbitrary",)),
        name="combine",
    )(dest_flat, dest_flat, probs.reshape(n * TOP_K), x1, gpost, g2, y_tiles)


def _layer(x2, c_pad, batch, seq, norm_mix_pre, norm_mix_post, norm_ffn_pre, norm_ffn_post,
           w_ada, b_ada, w_in, rel_bias, w_attn_out, w_dw, b_dw, conv_ln_g, conv_ln_b,
           w_conv_out, w_out, w_router, b_router, w_gate_up, b_gate_up, w_down, b_down):
    n, d = x2.shape
    attn_width = w_attn_out.shape[0]
    conv_width = w_conv_out.shape[0]
    taps = w_dw.shape[0]
    max_rel = (rel_bias.shape[1] - 1) // 2
    ne = w_router.shape[1]
    assert d == ROW_TILE * LANES, "row tiles assume 16 * 128 features per token"
    row = lambda v: v.reshape(1, -1)

    mod = _ada(c_pad, w_ada, row(b_ada))[:batch]
    sh1, sc1, g1, sh2, sc2, g2 = [m.reshape(batch, 1, d) for m in jnp.split(mod, 6, axis=-1)]

    proj = _inproj(x2, row(norm_mix_pre), sc1, sh1, w_in.astype(BF16), seq)

    g, masks = _attn_tables(rel_bias, ATTN_TQ, LEFT_CHUNKS * CHUNK, max_rel)
    attn = _attention(proj, g, masks, batch, seq, attn_width, 0)

    col_a = 3 * attn_width // conv_width
    w_dw_pad = jnp.pad(w_dw, ((0, CONV_HALO - taps), (0, 0)))
    cv = _conv(proj, w_dw_pad, row(b_dw), row(conv_ln_g), row(conv_ln_b),
               batch, seq, col_a, col_a + 1, conv_width, taps)

    col_g = (3 * attn_width + 2 * conv_width) // (d // 2)
    x1, h_tiles, logits = _mix(
        attn, cv, proj, x2, w_attn_out.astype(BF16), w_conv_out.astype(BF16),
        w_out.astype(BF16), row(norm_mix_post), g1, row(norm_ffn_pre), sc2, sh2,
        w_router.astype(BF16), row(b_router), seq, col_g)

    idx, probs, counts_f = _topk(logits)
    counts = counts_f.reshape(ne).astype(I32)
    padded = ((counts + MOE_TM - 1) // MOE_TM) * MOE_TM
    ends = jnp.cumsum(padded)
    starts = ends - padded
    n_rows = n * TOP_K + ne * MOE_TM
    n_tiles = n_rows // MOE_TM
    tile_start = jnp.arange(n_tiles, dtype=I32) * MOE_TM
    tile_e = jnp.minimum(jnp.sum((tile_start[:, None] >= ends[None, :]).astype(I32), axis=1),
                         ne - 1)
    n_used = (ends[-1:] // MOE_TM).astype(I32)
    experts = jnp.arange(ne, dtype=I32)
    nonempty = counts > 0
    later = nonempty[None, :] & (experts[None, :] > experts[:, None])
    next_e = jnp.min(jnp.where(later, experts[None, :], ne), axis=1)
    next_e = jnp.where(next_e < ne, next_e, jnp.min(jnp.where(nonempty, experts, ne)))
    ordinal_e = jnp.cumsum(nonempty.astype(I32)) - 1
    n_groups = jnp.sum(nonempty.astype(I32), keepdims=True)
    tiles = (tile_e, next_e[tile_e].astype(I32), ordinal_e[tile_e].astype(I32),
             jnp.concatenate([n_used, n_groups]))

    dest = _slots(idx, starts.astype(F32).reshape(1, ne))
    dest_flat = dest.reshape(n * TOP_K)
    xs = _scatter_rows(dest_flat, counts, starts, n_used, h_tiles, n_rows)
    act = _gate_up(tiles, xs, w_gate_up, b_gate_up.reshape(ne, 1, -1))
    y_tiles = _down(tiles, act, w_down, b_down.reshape(ne, 1, -1))
    return _combine(dest_flat, probs, x1, row(norm_ffn_post), g2, y_tiles, seq)


def kernel(x, c, norm_mix_pre, norm_mix_post, norm_ffn_pre, norm_ffn_post, w_ada, b_ada, w_in,
           rel_bias, w_attn_out, w_dw, b_dw, conv_ln_g, conv_ln_b, w_conv_out, w_out, w_router,
           b_router, w_gate_up, b_gate_up, w_down, b_down):
    batch, seq, d = x.shape
    x2 = x.reshape(batch * seq, d)
    c_pad = jnp.pad(c, ((0, SUBLANES - batch % SUBLANES), (0, 0))) if batch % SUBLANES else c
    for l in range(w_ada.shape[0]):
        x2 = _layer(x2, c_pad, batch, seq, norm_mix_pre[l], norm_mix_post[l], norm_ffn_pre[l],
                    norm_ffn_post[l], w_ada[l], b_ada[l], w_in[l], rel_bias[l], w_attn_out[l],
                    w_dw[l], b_dw[l], conv_ln_g[l], conv_ln_b[l], w_conv_out[l], w_out[l],
                    w_router[l], b_router[l], w_gate_up[l], b_gate_up[l], w_down[l], b_down[l])
    return x2.reshape(batch, seq, d)
```
